```python
import math
import jax, jax.numpy as jnp
from jax import lax
import numpy as np

D_MODEL = 2048
BATCH = 4
SEQ = 4096
DEPTH = 4

GRID_W = 64
CTX_LEN = 256
N_MIXERS = 3
DA_HEADS = 16
DA_HEAD_DIM = 64
DA_WIDTH = DA_HEADS * 2 * DA_HEAD_DIM
ROPE_BASE = 10000.0
ROPE_FREQS = DA_HEAD_DIM // 4
Q_BLOCK = 128
CF_WIDTH = D_MODEL
CF_KERNEL = 31
SC_WIDTH = D_MODEL
SC_KERNEL = 3
NORM_EPS = 1e-6
LN_EPS = 1e-5

kernel_name = "hybrid_diffattn_conformer_shortconv_dit"


def rmsnorm(x, g):
    xf = x.astype(jnp.float32)
    y = xf * lax.rsqrt(jnp.mean(xf * xf, axis=-1, keepdims=True) + NORM_EPS)
    return (y * g.astype(jnp.float32)).astype(x.dtype)


def layernorm(x, g, b):
    xf = x.astype(jnp.float32)
    mu = jnp.mean(xf, axis=-1, keepdims=True)
    var = jnp.mean(jnp.square(xf - mu), axis=-1, keepdims=True)
    y = (xf - mu) * lax.rsqrt(var + LN_EPS)
    return (y * g.astype(jnp.float32) + b.astype(jnp.float32)).astype(x.dtype)


def adaln(cond, w, b):
    m = jax.nn.silu(cond) @ w + b
    return jnp.split(m, 3, axis=-1)


def dwconv(x, w):
    k = w.shape[0]
    return lax.conv_general_dilated(
        x, w[:, None, :].astype(x.dtype), window_strides=(1,),
        padding=[(k // 2, k // 2)], dimension_numbers=("NWC", "WIO", "NWC"),
        feature_group_count=x.shape[-1])


def axial_rope_tables(rows):
    row = jnp.repeat(jnp.arange(rows, dtype=jnp.float32), GRID_W)
    col = jnp.tile(jnp.arange(GRID_W, dtype=jnp.float32), rows)
    inv = ROPE_BASE ** (-jnp.arange(ROPE_FREQS, dtype=jnp.float32) / ROPE_FREQS)
    ang = jnp.stack([row[:, None] * inv, col[:, None] * inv], axis=1)
    return jnp.cos(ang), jnp.sin(ang)


def apply_rope(x, cos, sin):
    xs = x.reshape(x.shape[:-1] + (2, 2, ROPE_FREQS))
    x1, x2 = xs[..., 0, :], xs[..., 1, :]
    c = cos[:, None, None].astype(x.dtype)
    s = sin[:, None, None].astype(x.dtype)
    out = jnp.stack([x1 * c - x2 * s, x2 * c + x1 * s], axis=-2)
    return out.reshape(x.shape)


def diff_attention(hl, hc, p, layer_idx, cos, sin, keep_ctx):
    B, S, _ = hl.shape
    L = hc.shape[1]
    H, dh = DA_HEADS, DA_HEAD_DIM
    scale = 1.0 / math.sqrt(dh)
    lam_init = 0.8 - 0.6 * math.exp(-0.3 * layer_idx)
    f32 = jnp.float32
    lam = (jnp.exp(jnp.sum(p["lam_q1"].astype(f32) * p["lam_k1"].astype(f32)))
           - jnp.exp(jnp.sum(p["lam_q2"].astype(f32) * p["lam_k2"].astype(f32))) + lam_init)
    w_in = p["w_in"]

    ql, kl, vl, zl = jnp.split(hl @ w_in, 4, axis=-1)
    ql = apply_rope(ql.reshape(B, S, H, 2, dh), cos, sin)
    kl = apply_rope(kl.reshape(B, S, H, 2, dh), cos, sin)
    vl = vl.reshape(B, S, H, 2 * dh)

    if keep_ctx:
        qc, kc, vc, zc = jnp.split(hc @ w_in, 4, axis=-1)
        qc = qc.reshape(B, L, H, 2, dh)
    else:
        kc, vc = jnp.split(hc @ w_in[:, DA_WIDTH:3 * DA_WIDTH], 2, axis=-1)
    kc = kc.reshape(B, L, H, 2, dh)
    vc = vc.reshape(B, L, H, 2 * dh)

    k_all = jnp.concatenate([kc, kl], axis=1)
    v_all = jnp.concatenate([vc, vl], axis=1)

    def attend(q, k, v):
        s = jnp.einsum("bqhcd,bkhcd->bhcqk", q, k).astype(f32) * scale
        pr = jax.nn.softmax(s, axis=-1).astype(v.dtype)
        o = jnp.einsum("bhcqk,bkhe->bqhce", pr, v)
        return o[..., 0, :] - lam.astype(o.dtype) * o[..., 1, :]

    def finish(o, z):
        n = o.shape[1]
        on = rmsnorm(o, p["head_g"]) * (1.0 - lam_init)
        return (on.reshape(B, n, DA_WIDTH) * jax.nn.silu(z)) @ p["w_out"]

    nb = S // Q_BLOCK
    qb = ql.reshape(B, nb, Q_BLOCK, H, 2, dh).transpose(1, 0, 2, 3, 4, 5)
    ol = lax.map(lambda q: attend(q, k_all, v_all), qb)
    ol = ol.transpose(1, 0, 2, 3, 4).reshape(B, S, H, 2 * dh)
    yl = finish(ol, zl)
    yc = finish(attend(qc, kc, vc), zc) if keep_ctx else None
    return yl, yc


def conformer_conv(h, p):
    a, b, z = jnp.split(h @ p["w_in"], 3, axis=-1)
    u = a * jax.nn.sigmoid(b)
    u = dwconv(u, p["dw_w"]) + p["dw_b"]
    u = jax.nn.silu(layernorm(u, p["ln_g"], p["ln_b"]))
    return (u * jax.nn.silu(z)) @ p["w_out"]


def short_conv(h, p):
    bg, cg, v, z = jnp.split(h @ p["w_in"], 4, axis=-1)
    y = bg * dwconv(cg * v, p["conv_w"])
    return (y * jax.nn.silu(z)) @ p["w_out"]


def setup_inputs(seed: int = 0) -> dict:
    key = jax.random.key(seed)
    ks = iter(jax.random.split(key, 96))
    d = D_MODEL

    def nrm(shape, s):
        return jax.random.normal(next(ks), shape, jnp.float32) * s

    inp = {}
    inp["x"] = nrm((BATCH, SEQ, d), 1.0)
    inp["c"] = nrm((BATCH, d), 1.0)
    inp["ctx"] = nrm((BATCH, CTX_LEN, d), 1.0)
    inp["c_ctx"] = nrm((d,), 1.0)

    def common(pre):
        inp[pre + "norm_g"] = 1.0 + nrm((d,), 0.02)
        inp[pre + "ada_w"] = nrm((d, 3 * d), 0.5 * d ** -0.5)
        inp[pre + "ada_b"] = nrm((3 * d,), 0.02)

    def attn(pre):
        common(pre)
        inp[pre + "w_in"] = nrm((d, 4 * DA_WIDTH), d ** -0.5)
        inp[pre + "lam_q1"] = nrm((DA_HEAD_DIM,), 0.1)
        inp[pre + "lam_k1"] = nrm((DA_HEAD_DIM,), 0.1)
        inp[pre + "lam_q2"] = nrm((DA_HEAD_DIM,), 0.1)
        inp[pre + "lam_k2"] = nrm((DA_HEAD_DIM,), 0.1)
        inp[pre + "head_g"] = 1.0 + nrm((2 * DA_HEAD_DIM,), 0.02)
        inp[pre + "w_out"] = nrm((DA_WIDTH, d), DA_WIDTH ** -0.5)

    def conformer(pre):
        common(pre)
        inp[pre + "w_in"] = nrm((d, 3 * CF_WIDTH), d ** -0.5)
        inp[pre + "dw_w"] = nrm((CF_KERNEL, CF_WIDTH), CF_KERNEL ** -0.5)
        inp[pre + "dw_b"] = nrm((CF_WIDTH,), 0.02)
        inp[pre + "ln_g"] = 1.0 + nrm((CF_WIDTH,), 0.02)
        inp[pre + "ln_b"] = nrm((CF_WIDTH,), 0.02)
        inp[pre + "w_out"] = nrm((CF_WIDTH, d), CF_WIDTH ** -0.5)

    def shortconv(pre):
        common(pre)
        inp[pre + "w_in"] = nrm((d, 4 * SC_WIDTH), d ** -0.5)
        inp[pre + "conv_w"] = nrm((SC_KERNEL, SC_WIDTH), SC_KERNEL ** -0.5)
        inp[pre + "w_out"] = nrm((SC_WIDTH, d), SC_WIDTH ** -0.5)

    builders = (attn, conformer, shortconv)
    for i in range(DEPTH):
        builders[i % N_MIXERS]("l%d_" % i)
    inp["final_norm_g"] = 1.0 + nrm((d,), 0.02)
    return inp


def reference(x, c, ctx, c_ctx,
              l0_norm_g, l0_ada_w, l0_ada_b, l0_w_in, l0_lam_q1, l0_lam_k1, l0_lam_q2, l0_lam_k2, l0_head_g, l0_w_out,
              l1_norm_g, l1_ada_w, l1_ada_b, l1_w_in, l1_dw_w, l1_dw_b, l1_ln_g, l1_ln_b, l1_w_out,
              l2_norm_g, l2_ada_w, l2_ada_b, l2_w_in, l2_conv_w, l2_w_out,
              l3_norm_g, l3_ada_w, l3_ada_b, l3_w_in, l3_lam_q1, l3_lam_k1, l3_lam_q2, l3_lam_k2, l3_head_g, l3_w_out,
              final_norm_g):
    S = x.shape[1]
    rows = S // GRID_W
    cos, sin = axial_rope_tables(rows)

    kinds = ("attn", "conformer", "shortconv")
    layers = [
        dict(norm_g=l0_norm_g, ada_w=l0_ada_w, ada_b=l0_ada_b, w_in=l0_w_in, lam_q1=l0_lam_q1, lam_k1=l0_lam_k1,
             lam_q2=l0_lam_q2, lam_k2=l0_lam_k2, head_g=l0_head_g, w_out=l0_w_out),
        dict(norm_g=l1_norm_g, ada_w=l1_ada_w, ada_b=l1_ada_b, w_in=l1_w_in, dw_w=l1_dw_w, dw_b=l1_dw_b,
             ln_g=l1_ln_g, ln_b=l1_ln_b, w_out=l1_w_out),
        dict(norm_g=l2_norm_g, ada_w=l2_ada_w, ada_b=l2_ada_b, w_in=l2_w_in, conv_w=l2_conv_w, w_out=l2_w_out),
        dict(norm_g=l3_norm_g, ada_w=l3_ada_w, ada_b=l3_ada_b, w_in=l3_w_in, lam_q1=l3_lam_q1, lam_k1=l3_lam_k1,
             lam_q2=l3_lam_q2, lam_k2=l3_lam_k2, head_g=l3_head_g, w_out=l3_w_out),
    ]

    xl, xc = x, ctx
    for i in range(DEPTH):
        kind = kinds[i % N_MIXERS]
        p = layers[i]
        last = i == DEPTH - 1
        sh_l, sc_l, g_l = adaln(c, p["ada_w"], p["ada_b"])
        hl = rmsnorm(xl, p["norm_g"]) * (1.0 + sc_l[:, None]) + sh_l[:, None]
        need_ctx = (not last) or kind == "attn"
        if need_ctx:
            sh_c, sc_c, g_c = adaln(c_ctx, p["ada_w"], p["ada_b"])
            hc = rmsnorm(xc, p["norm_g"]) * (1.0 + sc_c) + sh_c
        if kind == "attn":
            yl, yc = diff_attention(hl, hc, p, i, cos, sin, keep_ctx=not last)
        elif kind == "conformer":
            yl = conformer_conv(hl, p)
            yc = conformer_conv(hc, p) if not last else None
        else:
            yl = short_conv(hl, p)
            yc = short_conv(hc, p) if not last else None
        xl = xl + g_l[:, None] * yl
        if not last:
            xc = xc + g_c * yc
    return rmsnorm(xl, final_norm_g)
```

```python
import functools
import math

import jax
import jax.numpy as jnp
from jax import lax
from jax.experimental import pallas as pl
from jax.experimental.pallas import tpu as pltpu

F32 = jnp.float32
BF16 = jnp.bfloat16

LANES = 128
HALO = 16
ROW_TILE = 256
DA_HEAD_DIM = 64
HEAD_W = 2 * DA_HEAD_DIM
ROPE_BASE = 10000.0
ROPE_FREQS = DA_HEAD_DIM // 4
GRID_W = 64
CF_KERNEL = 31
SC_KERNEL = 3
NORM_EPS = 1e-6
LN_EPS = 1e-5
VMEM_LIMIT = 52 * 1024 * 1024


def _cparams(*sem):
    return pltpu.CompilerParams(dimension_semantics=sem, vmem_limit_bytes=VMEM_LIMIT)


def _silu(x):
    return x * jax.nn.sigmoid(x)


def _ada_kernel(c_ref, w_ref, b_ref, o_ref):
    a = _silu(c_ref[...]).astype(BF16)
    o_ref[...] = jnp.dot(a, w_ref[...].astype(BF16), preferred_element_type=F32) + b_ref[...]


def _adaln(cond8, w, b):
    d, n = w.shape
    tn = min(1024, n)
    return pl.pallas_call(
        _ada_kernel,
        grid=(n // tn,),
        in_specs=[pl.BlockSpec((8, d), lambda j: (0, 0)),
                  pl.BlockSpec((d, tn), lambda j: (0, j)),
                  pl.BlockSpec((1, tn), lambda j: (0, j))],
        out_specs=pl.BlockSpec((8, tn), lambda j: (0, j)),
        out_shape=jax.ShapeDtypeStruct((8, n), F32),
        compiler_params=_cparams("parallel"),
        name="adaln",
    )(cond8, w, b.reshape(1, n))


def _mod_table(m, nb, d):
    sh, sc, g = m[:, :d], m[:, d:2 * d], m[:, 2 * d:]
    rows = jnp.stack([1.0 + sc, sh, g], axis=1)
    lat = rows[:nb]
    ctx = jnp.broadcast_to(rows[nb][None], (nb, 3, d))
    tab = jnp.stack([lat, ctx], axis=1)
    return jnp.pad(tab, ((0, 0), (0, 0), (0, 5), (0, 0)))


def _norm_mod(x, g, scale1p, shift):
    ms = jnp.mean(x * x, axis=-1, keepdims=True)
    y = x * lax.rsqrt(ms + NORM_EPS)
    return (y * g) * scale1p + shift


def _pro_kernel(x_ref, tab_ref, g_ref, h_ref):
    tab = tab_ref[0, 0]
    h_ref[0] = _norm_mod(x_ref[0], g_ref[...], tab[0:1], tab[1:2]).astype(BF16)


def _prologue(x, tab, norm_g, nlat):
    nb, r, d = x.shape
    return pl.pallas_call(
        _pro_kernel,
        grid=(nb, r // ROW_TILE),
        in_specs=[pl.BlockSpec((1, ROW_TILE, d), lambda b, t: (b, t, 0)),
                  pl.BlockSpec((1, 1, 8, d), lambda b, t: (b, t // nlat, 0, 0)),
                  pl.BlockSpec((1, d), lambda b, t: (0, 0))],
        out_specs=pl.BlockSpec((1, ROW_TILE, d), lambda b, t: (b, t, 0)),
        out_shape=jax.ShapeDtypeStruct((nb, r, d), BF16),
        compiler_params=_cparams("parallel", "parallel"),
        name="prologue",
    )(x, tab, norm_g.reshape(1, d))


def _in_kernel(h_ref, w_ref, o_ref):
    o_ref[...] = jnp.dot(h_ref[...], w_ref[...], preferred_element_type=F32).astype(BF16)


def _in_rope_kernel(h_ref, w_ref, cos_ref, sa_ref, sb_ref, o_ref, *, n_rope, tn):
    acc = jnp.dot(h_ref[...], w_ref[...], preferred_element_type=F32)
    j = pl.program_id(1)

    @pl.when(j < n_rope)
    def _():
        cos, sa, sb = cos_ref[...], sa_ref[...], sb_ref[...]
        for c in range(tn // LANES):
            xc = acc[:, c * LANES:(c + 1) * LANES]
            rot = (xc * cos + pltpu.roll(xc, LANES - ROPE_FREQS, 1) * sa
                   + pltpu.roll(xc, ROPE_FREQS, 1) * sb)
            o_ref[:, c * LANES:(c + 1) * LANES] = rot.astype(BF16)

    @pl.when(j >= n_rope)
    def _():
        o_ref[...] = acc.astype(BF16)


def _in_tiles(t, r, d, n):
    tm = r // 4
    tn = 1024 if d >= 1024 else d
    assert tm % HALO == 0 and t % tm == 0 and n % tn == 0
    return tm, tn


def _in_proj(h, w, r, rope=None, rope_cols=0):
    t, d = h.shape
    n = w.shape[1]
    tm, tn = _in_tiles(t, r, d, n)
    grid = (t // tm, n // tn)
    h_spec = pl.BlockSpec((tm, d), lambda i, j: (i, 0))
    w_spec = pl.BlockSpec((d, tn), lambda i, j: (0, j))
    o_spec = pl.BlockSpec((tm, tn), lambda i, j: (i, j))
    out_shape = jax.ShapeDtypeStruct((t, n), BF16)
    if rope is None:
        return pl.pallas_call(_in_kernel, grid=grid, in_specs=[h_spec, w_spec], out_specs=o_spec,
                              out_shape=out_shape, compiler_params=_cparams("parallel", "arbitrary"),
                              name="in_proj")(h, w)
    assert rope_cols % tn == 0 and tn % LANES == 0
    per = r // tm
    t_spec = pl.BlockSpec((tm, LANES), lambda i, j: (i % per, 0))
    kern = functools.partial(_in_rope_kernel, n_rope=rope_cols // tn, tn=tn)
    return pl.pallas_call(kern, grid=grid, in_specs=[h_spec, w_spec, t_spec, t_spec, t_spec],
                          out_specs=o_spec, out_shape=out_shape,
                          compiler_params=_cparams("parallel", "arbitrary"),
                          name="in_proj_rope")(h, w, *rope)


def _rope_tables(s, l):
    t = jnp.arange(s)
    row = (t // GRID_W).astype(F32)
    col = (t % GRID_W).astype(F32)
    inv = ROPE_BASE ** (-jnp.arange(ROPE_FREQS, dtype=F32) / ROPE_FREQS)
    lane = jnp.arange(LANES)
    dd = lane % DA_HEAD_DIM
    axis, half, f = dd // (2 * ROPE_FREQS), (dd % (2 * ROPE_FREQS)) // ROPE_FREQS, dd % ROPE_FREQS
    pos = jnp.where(axis[None, :] == 0, row[:, None], col[:, None])
    ang = pos * inv[f][None, :]
    cos, sin = jnp.cos(ang), jnp.sin(ang)
    sa = jnp.where(half[None, :] == 0, -sin, 0.0)
    sb = jnp.where(half[None, :] == 1, sin, 0.0)
    cos = jnp.concatenate([cos, jnp.ones((l, LANES), F32)], axis=0)
    sa = jnp.concatenate([sa, jnp.zeros((l, LANES), F32)], axis=0)
    sb = jnp.concatenate([sb, jnp.zeros((l, LANES), F32)], axis=0)
    return cos, sa, sb


V_ROWS = HEAD_W + HALO


def _attn_body(lam_ref, hg_ref, q_ref, k_ref, v_ref, z_ref, o_ref, vt_ref, *, lam_init):
    @pl.when(pl.program_id(2) == 0)
    def _():
        nkv = v_ref.shape[1]
        vt_ref[0:HEAD_W, :] = v_ref[0].astype(F32).T.astype(BF16)
        vt_ref[HEAD_W:V_ROWS, :] = jnp.ones((HALO, nkv), BF16)

    lp = lam_ref[...]
    lam = (jnp.exp(jnp.sum(lp[0:1] * lp[1:2], axis=-1, keepdims=True))
           - jnp.exp(jnp.sum(lp[2:3] * lp[3:4], axis=-1, keepdims=True)) + lam_init)

    q = q_ref[0] * (1.0 / math.sqrt(DA_HEAD_DIM))
    lane = lax.broadcasted_iota(jnp.int32, q.shape, 1)
    k = k_ref[0]
    outs = []
    for c in range(2):
        qc = jnp.where((lane // DA_HEAD_DIM) == c, q, jnp.zeros_like(q))
        st = lax.dot_general(k, qc, (((1,), (1,)), ((), ())), preferred_element_type=F32)
        m = jnp.max(st, axis=0, keepdims=True)
        p = jnp.exp(st - m).astype(BF16)
        ot = jnp.dot(vt_ref[...], p, preferred_element_type=F32)
        outs.append(ot[0:HEAD_W] * (1.0 / ot[HEAD_W:HEAD_W + 1]))
    d = outs[0] - lam * outs[1]
    ms = jnp.mean(d * d, axis=0, keepdims=True)
    dn = (d * lax.rsqrt(ms + NORM_EPS)).T
    on = (dn * hg_ref[...]) * (1.0 - lam_init)
    o_ref[0] = (on * _silu(z_ref[0].astype(F32))).astype(BF16)


def _attn_call(qkvz, lamp, hg, u_prev, *, nh, tq, q_blk0, nq, nkv, kv_blk, lam_init):
    nb, r, _ = qkvz.shape
    d = nh * HEAD_W
    kern = functools.partial(_attn_body, lam_init=lam_init)
    in_specs = [pl.BlockSpec((8, LANES), lambda b, h, i: (0, 0)),
                pl.BlockSpec((1, LANES), lambda b, h, i: (0, 0)),
                pl.BlockSpec((1, tq, HEAD_W), lambda b, h, i: (b, q_blk0 + i, h)),
                pl.BlockSpec((1, nkv, HEAD_W), lambda b, h, i: (b, kv_blk, nh + h)),
                pl.BlockSpec((1, nkv, HEAD_W), lambda b, h, i: (b, kv_blk, 2 * nh + h)),
                pl.BlockSpec((1, tq, HEAD_W), lambda b, h, i: (b, q_blk0 + i, 3 * nh + h))]
    args = [lamp, hg, qkvz, qkvz, qkvz, qkvz]
    aliases = {}
    if u_prev is not None:
        in_specs.append(pl.BlockSpec(memory_space=pl.ANY))
        args.append(u_prev)
        aliases = {6: 0}
        body = lambda a, b_, c, d_, e, f, _g, o, vt: kern(a, b_, c, d_, e, f, o, vt)
    else:
        body = kern
    return pl.pallas_call(
        body,
        grid=(nb, nh, nq),
        in_specs=in_specs,
        out_specs=pl.BlockSpec((1, tq, HEAD_W), lambda b, h, i: (b, q_blk0 + i, h)),
        out_shape=jax.ShapeDtypeStruct((nb, r, d), BF16),
        scratch_shapes=[pltpu.VMEM((V_ROWS, nkv), BF16)],
        input_output_aliases=aliases,
        compiler_params=_cparams("parallel", "parallel", "arbitrary"),
        name="diff_attn" if u_prev is None else "diff_attn_ctx",
    )(*args)


def _diff_attention(qkvz, p, layer_idx, s, l, keep_ctx):
    nb, r, n4 = qkvz.shape
    nh = n4 // (4 * HEAD_W)
    lam_init = 0.8 - 0.6 * math.exp(-0.3 * layer_idx)
    lamp = jnp.stack([p["lam_q1"], p["lam_k1"], p["lam_q2"], p["lam_k2"]]).astype(F32)
    lamp = jnp.pad(lamp, ((0, 4), (0, LANES - DA_HEAD_DIM)))
    hg = p["head_g"].astype(F32).reshape(1, HEAD_W)
    tq = ROW_TILE
    u = _attn_call(qkvz, lamp, hg, None, nh=nh, tq=tq, q_blk0=0, nq=s // tq, nkv=r, kv_blk=0,
                   lam_init=lam_init)
    if keep_ctx:
        u = _attn_call(qkvz, lamp, hg, u, nh=nh, tq=l, q_blk0=s // l, nq=1, nkv=l, kv_blk=s // l,
                       lam_init=lam_init)
    return u


def _out_kernel(u_ref, w_ref, x_ref, tg_ref, tn_ref, g_ref, xo_ref, ho_ref):
    y = jnp.dot(u_ref[0], w_ref[...], preferred_element_type=F32)
    xn = x_ref[0] + tg_ref[0, 0][2:3] * y
    xo_ref[0] = xn
    tab = tn_ref[0, 0]
    ho_ref[0] = _norm_mod(xn, g_ref[...], tab[0:1], tab[1:2]).astype(BF16)


def _out_final_kernel(u_ref, w_ref, x_ref, tg_ref, g_ref, o_ref):
    y = jnp.dot(u_ref[0], w_ref[...], preferred_element_type=F32)
    xn = x_ref[0] + tg_ref[0, 0][2:3] * y
    ms = jnp.mean(xn * xn, axis=-1, keepdims=True)
    o_ref[0] = (xn * lax.rsqrt(ms + NORM_EPS)) * g_ref[...]


def _out_proj(u, w, x, tab, tab_next, norm_g_next, nlat):
    nb, r, d = x.shape
    dw = u.shape[2]
    row = lambda b, t: (b, t, 0)
    tabspec = pl.BlockSpec((1, 1, 8, d), lambda b, t: (b, t // nlat, 0, 0))
    return pl.pallas_call(
        _out_kernel,
        grid=(nb, r // ROW_TILE),
        in_specs=[pl.BlockSpec((1, ROW_TILE, dw), row),
                  pl.BlockSpec((dw, d), lambda b, t: (0, 0)),
                  pl.BlockSpec((1, ROW_TILE, d), row),
                  tabspec, tabspec,
                  pl.BlockSpec((1, d), lambda b, t: (0, 0))],
        out_specs=[pl.BlockSpec((1, ROW_TILE, d), row), pl.BlockSpec((1, ROW_TILE, d), row)],
        out_shape=[jax.ShapeDtypeStruct((nb, r, d), F32), jax.ShapeDtypeStruct((nb, r, d), BF16)],
        compiler_params=_cparams("parallel", "parallel"),
        name="out_proj",
    )(u, w, x, tab, tab_next, norm_g_next.reshape(1, d))


def _out_proj_final(u, w, x, tab, final_g, s):
    nb, r, d = x.shape
    dw = u.shape[2]
    row = lambda b, t: (b, t, 0)
    return pl.pallas_call(
        _out_final_kernel,
        grid=(nb, s // ROW_TILE),
        in_specs=[pl.BlockSpec((1, ROW_TILE, dw), row),
                  pl.BlockSpec((dw, d), lambda b, t: (0, 0)),
                  pl.BlockSpec((1, ROW_TILE, d), row),
                  pl.BlockSpec((1, 1, 8, d), lambda b, t: (b, 0, 0, 0)),
                  pl.BlockSpec((1, d), lambda b, t: (0, 0))],
        out_specs=pl.BlockSpec((1, ROW_TILE, d), row),
        out_shape=jax.ShapeDtypeStruct((nb, s, d), F32),
        compiler_params=_cparams("parallel", "parallel"),
        name="out_proj_final",
    )(u, w, x, tab, final_g.reshape(1, d))


def _halo_valid(nlat):
    t = pl.program_id(1)
    prev_ok = jnp.logical_and(t != 0, t != nlat)
    next_ok = jnp.logical_and(t != nlat - 1, t != nlat)
    return prev_ok.astype(F32), next_ok.astype(F32)


def _conf_kernel(pa_ref, ca_ref, na_ref, pb_ref, cb_ref, nb_ref, z_ref, w_ref, wb_ref, lg_ref, lb_ref,
                 o_ref, win_ref, u_ref, *, nlat):
    tm = ca_ref.shape[1]
    nc = win_ref.shape[0]
    pv, nv = _halo_valid(nlat)

    def glu(a_ref, b_ref):
        return a_ref[0].astype(F32) * jax.nn.sigmoid(b_ref[0].astype(F32))

    gp, gc, gn = glu(pa_ref, pb_ref) * pv, glu(ca_ref, cb_ref), glu(na_ref, nb_ref) * nv
    for c in range(nc):
        sl = slice(c * LANES, (c + 1) * LANES)
        win_ref[c, 0:HALO, :] = gp[:, sl]
        win_ref[c, HALO:HALO + tm, :] = gc[:, sl]
        win_ref[c, HALO + tm:2 * HALO + tm, :] = gn[:, sl]

    off = HALO - CF_KERNEL // 2

    def conv_chunk(c, carry):
        w = w_ref[c]
        acc = jnp.zeros((tm, LANES), F32)
        for k in range(CF_KERNEL):
            acc = acc + win_ref[c, pl.ds(off + k, tm), :] * w[k:k + 1, :]
        u_ref[c] = acc + wb_ref[c][0:1]
        return carry

    lax.fori_loop(0, nc, conv_chunk, 0)

    d = nc * LANES
    s1 = u_ref[0]
    for c in range(1, nc):
        s1 = s1 + u_ref[c]
    mu = jnp.sum(s1, axis=-1, keepdims=True) / d
    s2 = jnp.square(u_ref[0] - mu)
    for c in range(1, nc):
        s2 = s2 + jnp.square(u_ref[c] - mu)
    rs = lax.rsqrt(jnp.sum(s2, axis=-1, keepdims=True) / d + LN_EPS)
    for c in range(nc):
        sl = slice(c * LANES, (c + 1) * LANES)
        y = ((u_ref[c] - mu) * rs) * lg_ref[:, sl] + lb_ref[:, sl]
        zc = z_ref[0, :, sl].astype(F32)
        o_ref[0, :, sl] = (_silu(y) * _silu(zc)).astype(BF16)


def _halo_specs(tm, d, col, r):
    nblk = tm // HALO
    last_blk = r // HALO - 1
    prev = pl.BlockSpec((1, HALO, d), lambda b, t: (b, jnp.maximum(t * nblk - 1, 0), col))
    cur = pl.BlockSpec((1, tm, d), lambda b, t: (b, t, col))
    nxt = pl.BlockSpec((1, HALO, d), lambda b, t: (b, jnp.minimum((t + 1) * nblk, last_blk), col))
    return [prev, cur, nxt]


def _conformer_mix(abz, p, s, l):
    nb, r, d3 = abz.shape
    d = d3 // 3
    tm = ROW_TILE
    assert l == tm and s % tm == 0 and d % LANES == 0
    nc = d // LANES
    nlat = s // tm
    specs = _halo_specs(tm, d, 0, r) + _halo_specs(tm, d, 1, r)
    z_spec = pl.BlockSpec((1, tm, d), lambda b, t: (b, t, 2))
    w = jnp.pad(p["dw_w"].astype(F32), ((0, 1), (0, 0))).reshape(CF_KERNEL + 1, nc, LANES).transpose(1, 0, 2)
    wb = jnp.broadcast_to(p["dw_b"].astype(F32).reshape(nc, 1, LANES), (nc, 8, LANES))
    full = lambda shape: pl.BlockSpec(shape, lambda b, t: (0,) * len(shape))
    kern = functools.partial(_conf_kernel, nlat=nlat)
    return pl.pallas_call(
        kern,
        grid=(nb, r // tm),
        in_specs=specs + [z_spec, full((nc, CF_KERNEL + 1, LANES)), full((nc, 8, LANES)),
                          full((1, d)), full((1, d))],
        out_specs=pl.BlockSpec((1, tm, d), lambda b, t: (b, t, 0)),
        out_shape=jax.ShapeDtypeStruct((nb, r, d), BF16),
        scratch_shapes=[pltpu.VMEM((nc, tm + 2 * HALO, LANES), F32), pltpu.VMEM((nc, tm, LANES), F32)],
        compiler_params=_cparams("parallel", "parallel"),
        name="conformer_mix",
    )(abz, abz, abz, abz, abz, abz, abz, w, wb, p["ln_g"].astype(F32).reshape(1, d),
      p["ln_b"].astype(F32).reshape(1, d))


def _sc_kernel(bg_ref, pc_ref, cc_ref, nc_ref, pv_ref, cv_ref, nv_ref, z_ref, w_ref, o_ref, win_ref, *, nlat):
    tm = cc_ref.shape[1]
    pvalid, nvalid = _halo_valid(nlat)

    def prod(c_ref, v_ref):
        return c_ref[0].astype(F32) * v_ref[0].astype(F32)

    win_ref[0:HALO, :] = prod(pc_ref, pv_ref) * pvalid
    win_ref[HALO:HALO + tm, :] = prod(cc_ref, cv_ref)
    win_ref[HALO + tm:2 * HALO + tm, :] = prod(nc_ref, nv_ref) * nvalid
    off = HALO - SC_KERNEL // 2
    w = w_ref[...]
    y = win_ref[pl.ds(off, tm), :] * w[0:1]
    for k in range(1, SC_KERNEL):
        y = y + win_ref[pl.ds(off + k, tm), :] * w[k:k + 1]
    o_ref[0] = ((bg_ref[0].astype(F32) * y) * _silu(z_ref[0].astype(F32))).astype(BF16)


def _shortconv_mix(bcvz, p, s, l):
    nb, r, d4 = bcvz.shape
    d = d4 // 4
    tm = ROW_TILE
    assert l == tm and s % tm == 0
    nlat = s // tm
    specs = [pl.BlockSpec((1, tm, d), lambda b, t: (b, t, 0))]
    specs += _halo_specs(tm, d, 1, r) + _halo_specs(tm, d, 2, r)
    specs.append(pl.BlockSpec((1, tm, d), lambda b, t: (b, t, 3)))
    specs.append(pl.BlockSpec((8, d), lambda b, t: (0, 0)))
    w = jnp.pad(p["conv_w"].astype(F32), ((0, 8 - SC_KERNEL), (0, 0)))
    kern = functools.partial(_sc_kernel, nlat=nlat)
    return pl.pallas_call(
        kern,
        grid=(nb, r // tm),
        in_specs=specs,
        out_specs=pl.BlockSpec((1, tm, d), lambda b, t: (b, t, 0)),
        out_shape=jax.ShapeDtypeStruct((nb, r, d), BF16),
        scratch_shapes=[pltpu.VMEM((tm + 2 * HALO, d), F32)],
        compiler_params=_cparams("parallel", "parallel"),
        name="shortconv_mix",
    )(bcvz, bcvz, bcvz, bcvz, bcvz, bcvz, bcvz, bcvz, w)


def _forward(x, c, ctx, c_ctx, layers, final_norm_g):
    nb, s, d = x.shape
    l = ctx.shape[1]
    r = s + l
    assert l == ROW_TILE and s % ROW_TILE == 0 and nb < 8
    nlat = s // ROW_TILE
    depth = len(layers)
    kinds = ("attn", "conformer", "shortconv")

    cond8 = jnp.zeros((8, d), F32).at[:nb].set(c).at[nb].set(c_ctx)
    tabs = [_mod_table(_adaln(cond8, p["ada_w"], p["ada_b"]), nb, d) for p in layers]
    rope = _rope_tables(s, l)

    xs = jnp.concatenate([x, ctx], axis=1)
    h = _prologue(xs, tabs[0], layers[0]["norm_g"], nlat)
    for i, p in enumerate(layers):
        kind = kinds[i % len(kinds)]
        last = i == depth - 1
        w_in = p["w_in"].astype(BF16)
        w_out = p["w_out"].astype(BF16)
        h2 = h.reshape(nb * r, d)
        if kind == "attn":
            qkvz = _in_proj(h2, w_in, r, rope=rope, rope_cols=w_in.shape[1] // 2).reshape(nb, r, -1)
            u = _diff_attention(qkvz, p, i, s, l, keep_ctx=not last)
        elif kind == "conformer":
            u = _conformer_mix(_in_proj(h2, w_in, r).reshape(nb, r, -1), p, s, l)
        else:
            u = _shortconv_mix(_in_proj(h2, w_in, r).reshape(nb, r, -1), p, s, l)
        if last:
            return _out_proj_final(u, w_out, xs, tabs[i], final_norm_g, s)
        xs, h = _out_proj(u, w_out, xs, tabs[i], tabs[i + 1], layers[i + 1]["norm_g"], nlat)


def kernel(x, c, ctx, c_ctx, l0_norm_g, l0_ada_w, l0_ada_b, l0_w_in, l0_lam_q1, l0_lam_k1, l0_lam_q2, l0_lam_k2, l0_head_g, l0_w_out, l1_norm_g, l1_ada_w, l1_ada_b, l1_w_in, l1_dw_w, l1_dw_b, l1_ln_g, l1_ln_b, l1_w_out, l2_norm_g, l2_ada_w, l2_ada_b, l2_w_in, l2_conv_w, l2_w_out, l3_norm_g, l3_ada_w, l3_ada_b, l3_w_in, l3_lam_q1, l3_lam_k1, l3_lam_q2, l3_lam_k2, l3_head_g, l3_w_out, final_norm_g):
    layers = [
        dict(norm_g=l0_norm_g, ada_w=l0_ada_w, ada_b=l0_ada_b, w_in=l0_w_in, lam_q1=l0_lam_q1, lam_k1=l0_lam_k1,
             lam_q2=l0_lam_q2, lam_k2=l0_lam_k2, head_g=l0_head_g, w_out=l0_w_out),
        dict(norm_g=l1_norm_g, ada_w=l1_ada_w, ada_b=l1_ada_b, w_in=l1_w_in, dw_w=l1_dw_w, dw_b=l1_dw_b,
             ln_g=l1_ln_g, ln_b=l1_ln_b, w_out=l1_w_out),
        dict(norm_g=l2_norm_g, ada_w=l2_ada_w, ada_b=l2_ada_b, w_in=l2_w_in, conv_w=l2_conv_w, w_out=l2_w_out),
        dict(norm_g=l3_norm_g, ada_w=l3_ada_w, ada_b=l3_ada_b, w_in=l3_w_in, lam_q1=l3_lam_q1, lam_k1=l3_lam_k1,
             lam_q2=l3_lam_q2, lam_k2=l3_lam_k2, head_g=l3_head_g, w_out=l3_w_out),
    ]
    return _forward(x, c, ctx, c_ctx, layers, final_norm_g)
```

```python
import functools
import math

import jax
import jax.numpy as jnp
from jax import lax
from jax.experimental import pallas as pl
from jax.experimental.pallas import tpu as pltpu

F32 = jnp.float32
BF16 = jnp.bfloat16

LANES = 128
HALO = 16
ROW_TILE = 256
DA_HEAD_DIM = 64
HEAD_W = 2 * DA_HEAD_DIM
ROPE_BASE = 10000.0
ROPE_FREQS = DA_HEAD_DIM // 4
GRID_W = 64
CF_KERNEL = 31
SC_KERNEL = 3
NORM_EPS = 1e-6
LN_EPS = 1e-5
Q_TILE = 512
KV_CHUNK = 256
Q_SCALE = math.log2(math.e) / math.sqrt(DA_HEAD_DIM)
VMEM_LIMIT = 52 * 1024 * 1024


def _cparams(*sem):
    return pltpu.CompilerParams(dimension_semantics=sem, vmem_limit_bytes=VMEM_LIMIT)


def _silu(x):
    return x * jax.nn.sigmoid(x)


def _ada_kernel(c_ref, w_ref, b_ref, o_ref):
    a = _silu(c_ref[...]).astype(BF16)
    o_ref[...] = jnp.dot(a, w_ref[...].astype(BF16), preferred_element_type=F32) + b_ref[...]


def _adaln(cond8, w, b):
    d, n = w.shape
    tn = min(1024, n)
    return pl.pallas_call(
        _ada_kernel,
        grid=(n // tn,),
        in_specs=[pl.BlockSpec((8, d), lambda j: (0, 0)),
                  pl.BlockSpec((d, tn), lambda j: (0, j)),
                  pl.BlockSpec((1, tn), lambda j: (0, j))],
        out_specs=pl.BlockSpec((8, tn), lambda j: (0, j)),
        out_shape=jax.ShapeDtypeStruct((8, n), F32),
        compiler_params=_cparams("parallel"),
        name="adaln",
    )(cond8, w, b.reshape(1, n))


def _mod_table(m, nb, d):
    sh, sc, g = m[:, :d], m[:, d:2 * d], m[:, 2 * d:]
    rows = jnp.stack([1.0 + sc, sh, g], axis=1)
    lat = rows[:nb]
    ctx = jnp.broadcast_to(rows[nb][None], (nb, 3, d))
    tab = jnp.stack([lat, ctx], axis=1)
    return jnp.pad(tab, ((0, 0), (0, 0), (0, 5), (0, 0)))


def _norm_mod(x, g, scale1p, shift):
    ms = jnp.mean(x * x, axis=-1, keepdims=True)
    y = x * lax.rsqrt(ms + NORM_EPS)
    return (y * g) * scale1p + shift


def _pro_kernel(x_ref, tab_ref, g_ref, h_ref):
    tab = tab_ref[0, 0]
    h_ref[0] = _norm_mod(x_ref[0], g_ref[...], tab[0:1], tab[1:2]).astype(BF16)


def _prologue(x, tab, norm_g, nlat):
    nb, r, d = x.shape
    return pl.pallas_call(
        _pro_kernel,
        grid=(nb, r // ROW_TILE),
        in_specs=[pl.BlockSpec((1, ROW_TILE, d), lambda b, t: (b, t, 0)),
                  pl.BlockSpec((1, 1, 8, d), lambda b, t: (b, t // nlat, 0, 0)),
                  pl.BlockSpec((1, d), lambda b, t: (0, 0))],
        out_specs=pl.BlockSpec((1, ROW_TILE, d), lambda b, t: (b, t, 0)),
        out_shape=jax.ShapeDtypeStruct((nb, r, d), BF16),
        compiler_params=_cparams("parallel", "parallel"),
        name="prologue",
    )(x, tab, norm_g.reshape(1, d))


def _in_kernel(h_ref, w_ref, o_ref):
    o_ref[...] = jnp.dot(h_ref[...], w_ref[...], preferred_element_type=F32).astype(BF16)


def _in_rope_kernel(h_ref, w_ref, cos_ref, sa_ref, sb_ref, o_ref, *, n_rope, tn, sub):
    j = pl.program_id(1)
    tm = h_ref.shape[0]

    @pl.when(j < n_rope)
    def _():
        for r0 in range(0, tm, sub):
            rows = slice(r0, r0 + sub)
            acc = jnp.dot(h_ref[rows, :], w_ref[...], preferred_element_type=F32)
            cos, sa, sb = cos_ref[rows, :], sa_ref[rows, :], sb_ref[rows, :]
            for c in range(tn // LANES):
                xc = acc[:, c * LANES:(c + 1) * LANES]
                rot = (xc * cos + pltpu.roll(xc, LANES - ROPE_FREQS, 1) * sa
                       + pltpu.roll(xc, ROPE_FREQS, 1) * sb)
                o_ref[rows, c * LANES:(c + 1) * LANES] = rot.astype(BF16)

    @pl.when(j >= n_rope)
    def _():
        o_ref[...] = jnp.dot(h_ref[...], w_ref[...], preferred_element_type=F32).astype(BF16)


def _in_tiles(t, r, d, n):
    tm = r // 4
    tn = 1024 if d >= 1024 else d
    assert tm % HALO == 0 and t % tm == 0 and n % tn == 0
    return tm, tn


def _in_proj(h, w, r, rope=None, rope_cols=0):
    t, d = h.shape
    n = w.shape[1]
    tm, tn = _in_tiles(t, r, d, n)
    grid = (t // tm, n // tn)
    h_spec = pl.BlockSpec((tm, d), lambda i, j: (i, 0))
    w_spec = pl.BlockSpec((d, tn), lambda i, j: (0, j))
    o_spec = pl.BlockSpec((tm, tn), lambda i, j: (i, j))
    out_shape = jax.ShapeDtypeStruct((t, n), BF16)
    if rope is None:
        return pl.pallas_call(_in_kernel, grid=grid, in_specs=[h_spec, w_spec], out_specs=o_spec,
                              out_shape=out_shape, compiler_params=_cparams("parallel", "arbitrary"),
                              name="in_proj")(h, w)
    assert rope_cols % (2 * tn) == 0 and tn % LANES == 0
    per = r // tm
    n_q = rope_cols // (2 * tn)
    t_spec = pl.BlockSpec((tm, LANES), lambda i, j: (i % per + jnp.where(j >= n_q, per, 0), 0))
    sub = tm // 4 if (tm // 4) % HALO == 0 else tm
    kern = functools.partial(_in_rope_kernel, n_rope=rope_cols // tn, tn=tn, sub=sub)
    return pl.pallas_call(kern, grid=grid, in_specs=[h_spec, w_spec, t_spec, t_spec, t_spec],
                          out_specs=o_spec, out_shape=out_shape,
                          compiler_params=_cparams("parallel", "arbitrary"),
                          name="in_proj_rope")(h, w, *rope)


def _rope_tables(s, l):
    t = jnp.arange(s)
    row = (t // GRID_W).astype(F32)
    col = (t % GRID_W).astype(F32)
    inv = ROPE_BASE ** (-jnp.arange(ROPE_FREQS, dtype=F32) / ROPE_FREQS)
    lane = jnp.arange(LANES)
    dd = lane % DA_HEAD_DIM
    axis, half, f = dd // (2 * ROPE_FREQS), (dd % (2 * ROPE_FREQS)) // ROPE_FREQS, dd % ROPE_FREQS
    pos = jnp.where(axis[None, :] == 0, row[:, None], col[:, None])
    ang = pos * inv[f][None, :]
    cos, sin = jnp.cos(ang), jnp.sin(ang)
    sa = jnp.where(half[None, :] == 0, -sin, 0.0)
    sb = jnp.where(half[None, :] == 1, sin, 0.0)
    cos = jnp.concatenate([cos, jnp.ones((l, LANES), F32)], axis=0)
    sa = jnp.concatenate([sa, jnp.zeros((l, LANES), F32)], axis=0)
    sb = jnp.concatenate([sb, jnp.zeros((l, LANES), F32)], axis=0)
    return tuple(jnp.concatenate([t * Q_SCALE, t], axis=0) for t in (cos, sa, sb))


V_ROWS = HEAD_W + HALO


def _attn_body(lam_ref, hg_ref, q_ref, k_ref, v_ref, z_ref, o_ref, vt_ref, *, lam_init, ck):
    @pl.when(pl.program_id(2) == 0)
    def _():
        nkv = v_ref.shape[1]
        vt_ref[0:HEAD_W, :] = v_ref[0].astype(F32).T.astype(BF16)
        vt_ref[HEAD_W:V_ROWS, :] = jnp.ones((HALO, nkv), BF16)

    lp = lam_ref[...]
    lam = (jnp.exp(jnp.sum(lp[0:1] * lp[1:2], axis=-1, keepdims=True))
           - jnp.exp(jnp.sum(lp[2:3] * lp[3:4], axis=-1, keepdims=True)) + lam_init)

    q = q_ref[0]
    tq = q.shape[0]
    nkv = k_ref.shape[1]
    lane = lax.broadcasted_iota(jnp.int32, q.shape, 1)
    q01 = jnp.concatenate([jnp.where((lane // DA_HEAD_DIM) == c, q, jnp.zeros_like(q)) for c in range(2)], axis=0)

    def scores(j):
        kj = k_ref[0, j * ck:(j + 1) * ck, :]
        return lax.dot_general(kj, q01, (((1,), (1,)), ((), ())), preferred_element_type=F32)

    s_next = scores(0)
    m = acc = None
    for j in range(nkv // ck):
        s = s_next
        if (j + 1) * ck < nkv:
            s_next = scores(j + 1)
        mc = jnp.max(s, axis=0, keepdims=True)
        m_new = mc if m is None else jnp.maximum(m, mc)
        p = jnp.exp2(s - m_new).astype(BF16)
        pv = jnp.dot(vt_ref[:, j * ck:(j + 1) * ck], p, preferred_element_type=F32)
        acc = pv if m is None else acc * jnp.exp2(m - m_new) + pv
        m = m_new
    o = acc[0:HEAD_W] * (1.0 / acc[HEAD_W:HEAD_W + 1])
    d = o[:, 0:tq] - lam * o[:, tq:2 * tq]
    ms = jnp.mean(d * d, axis=0, keepdims=True)
    dn = (d * lax.rsqrt(ms + NORM_EPS)).T
    on = (dn * hg_ref[...]) * (1.0 - lam_init)
    o_ref[0] = (on * _silu(z_ref[0].astype(F32))).astype(BF16)


def _attn_call(qkvz, lamp, hg, u_prev, *, nh, tq, q_blk0, nq, nkv, kv_blk, lam_init):
    nb, r, _ = qkvz.shape
    d = nh * HEAD_W
    kern = functools.partial(_attn_body, lam_init=lam_init, ck=KV_CHUNK)
    in_specs = [pl.BlockSpec((8, LANES), lambda b, h, i: (0, 0)),
                pl.BlockSpec((1, LANES), lambda b, h, i: (0, 0)),
                pl.BlockSpec((1, tq, HEAD_W), lambda b, h, i: (b, q_blk0 + i, h)),
                pl.BlockSpec((1, nkv, HEAD_W), lambda b, h, i: (b, kv_blk, nh + h)),
                pl.BlockSpec((1, nkv, HEAD_W), lambda b, h, i: (b, kv_blk, 2 * nh + h)),
                pl.BlockSpec((1, tq, HEAD_W), lambda b, h, i: (b, q_blk0 + i, 3 * nh + h))]
    args = [lamp, hg, qkvz, qkvz, qkvz, qkvz]
    aliases = {}
    if u_prev is not None:
        in_specs.append(pl.BlockSpec(memory_space=pl.ANY))
        args.append(u_prev)
        aliases = {6: 0}
        body = lambda a, b_, c, d_, e, f, _g, o, vt: kern(a, b_, c, d_, e, f, o, vt)
    else:
        body = kern
    return pl.pallas_call(
        body,
        grid=(nb, nh, nq),
        in_specs=in_specs,
        out_specs=pl.BlockSpec((1, tq, HEAD_W), lambda b, h, i: (b, q_blk0 + i, h)),
        out_shape=jax.ShapeDtypeStruct((nb, r, d), BF16),
        scratch_shapes=[pltpu.VMEM((V_ROWS, nkv), BF16)],
        input_output_aliases=aliases,
        compiler_params=_cparams("parallel", "parallel", "arbitrary"),
        name="diff_attn" if u_prev is None else "diff_attn_ctx",
    )(*args)


def _diff_attention(qkvz, p, layer_idx, s, l, keep_ctx):
    nb, r, n4 = qkvz.shape
    nh = n4 // (4 * HEAD_W)
    lam_init = 0.8 - 0.6 * math.exp(-0.3 * layer_idx)
    lamp = jnp.stack([p["lam_q1"], p["lam_k1"], p["lam_q2"], p["lam_k2"]]).astype(F32)
    lamp = jnp.pad(lamp, ((0, 4), (0, LANES - DA_HEAD_DIM)))
    hg = p["head_g"].astype(F32).reshape(1, HEAD_W)
    tq = min(Q_TILE, s)
    u = _attn_call(qkvz, lamp, hg, None, nh=nh, tq=tq, q_blk0=0, nq=s // tq, nkv=r, kv_blk=0,
                   lam_init=lam_init)
    if keep_ctx:
        u = _attn_call(qkvz, lamp, hg, u, nh=nh, tq=l, q_blk0=s // l, nq=1, nkv=l, kv_blk=s // l,
                       lam_init=lam_init)
    return u


def _out_kernel(u_ref, w_ref, x_ref, tg_ref, tn_ref, g_ref, xo_ref, ho_ref):
    y = jnp.dot(u_ref[0], w_ref[...], preferred_element_type=F32)
    xn = x_ref[0] + tg_ref[0, 0][2:3] * y
    xo_ref[0] = xn
    tab = tn_ref[0, 0]
    ho_ref[0] = _norm_mod(xn, g_ref[...], tab[0:1], tab[1:2]).astype(BF16)


def _out_final_kernel(u_ref, w_ref, x_ref, tg_ref, g_ref, o_ref):
    y = jnp.dot(u_ref[0], w_ref[...], preferred_element_type=F32)
    xn = x_ref[0] + tg_ref[0, 0][2:3] * y
    ms = jnp.mean(xn * xn, axis=-1, keepdims=True)
    o_ref[0] = (xn * lax.rsqrt(ms + NORM_EPS)) * g_ref[...]


def _out_proj(u, w, x, tab, tab_next, norm_g_next, nlat):
    nb, r, d = x.shape
    dw = u.shape[2]
    row = lambda b, t: (b, t, 0)
    tabspec = pl.BlockSpec((1, 1, 8, d), lambda b, t: (b, t // nlat, 0, 0))
    return pl.pallas_call(
        _out_kernel,
        grid=(nb, r // ROW_TILE),
        in_specs=[pl.BlockSpec((1, ROW_TILE, dw), row),
                  pl.BlockSpec((dw, d), lambda b, t: (0, 0)),
                  pl.BlockSpec((1, ROW_TILE, d), row),
                  tabspec, tabspec,
                  pl.BlockSpec((1, d), lambda b, t: (0, 0))],
        out_specs=[pl.BlockSpec((1, ROW_TILE, d), row), pl.BlockSpec((1, ROW_TILE, d), row)],
        out_shape=[jax.ShapeDtypeStruct((nb, r, d), F32), jax.ShapeDtypeStruct((nb, r, d), BF16)],
        compiler_params=_cparams("parallel", "parallel"),
        name="out_proj",
    )(u, w, x, tab, tab_next, norm_g_next.reshape(1, d))


def _out_proj_final(u, w, x, tab, final_g, s):
    nb, r, d = x.shape
    dw = u.shape[2]
    row = lambda b, t: (b, t, 0)
    return pl.pallas_call(
        _out_final_kernel,
        grid=(nb, s // ROW_TILE),
        in_specs=[pl.BlockSpec((1, ROW_TILE, dw), row),
                  pl.BlockSpec((dw, d), lambda b, t: (0, 0)),
                  pl.BlockSpec((1, ROW_TILE, d), row),
                  pl.BlockSpec((1, 1, 8, d), lambda b, t: (b, 0, 0, 0)),
                  pl.BlockSpec((1, d), lambda b, t: (0, 0))],
        out_specs=pl.BlockSpec((1, ROW_TILE, d), row),
        out_shape=jax.ShapeDtypeStruct((nb, s, d), F32),
        compiler_params=_cparams("parallel", "parallel"),
        name="out_proj_final",
    )(u, w, x, tab, final_g.reshape(1, d))


def _halo_valid(nlat):
    t = pl.program_id(1)
    prev_ok = jnp.logical_and(t != 0, t != nlat)
    next_ok = jnp.logical_and(t != nlat - 1, t != nlat)
    return prev_ok.astype(F32), next_ok.astype(F32)


def _conf_kernel(pa_ref, ca_ref, na_ref, pb_ref, cb_ref, nb_ref, z_ref, w_ref, wb_ref, lg_ref, lb_ref,
                 o_ref, win_ref, u_ref, *, nlat):
    tm = ca_ref.shape[1]
    nc = win_ref.shape[0]
    pv, nv = _halo_valid(nlat)

    def glu(a_ref, b_ref):
        return a_ref[0].astype(F32) * jax.nn.sigmoid(b_ref[0].astype(F32))

    gp, gc, gn = glu(pa_ref, pb_ref) * pv, glu(ca_ref, cb_ref), glu(na_ref, nb_ref) * nv
    for c in range(nc):
        sl = slice(c * LANES, (c + 1) * LANES)
        win_ref[c, 0:HALO, :] = gp[:, sl]
        win_ref[c, HALO:HALO + tm, :] = gc[:, sl]
        win_ref[c, HALO + tm:2 * HALO + tm, :] = gn[:, sl]

    off = HALO - CF_KERNEL // 2

    def conv_chunk(c, carry):
        w = w_ref[c]
        acc = jnp.zeros((tm, LANES), F32)
        for k in range(CF_KERNEL):
            acc = acc + win_ref[c, pl.ds(off + k, tm), :] * w[k:k + 1, :]
        u_ref[c] = acc + wb_ref[c][0:1]
        return carry

    lax.fori_loop(0, nc, conv_chunk, 0)

    d = nc * LANES
    s1 = u_ref[0]
    for c in range(1, nc):
        s1 = s1 + u_ref[c]
    mu = jnp.sum(s1, axis=-1, keepdims=True) / d
    s2 = jnp.square(u_ref[0] - mu)
    for c in range(1, nc):
        s2 = s2 + jnp.square(u_ref[c] - mu)
    rs = lax.rsqrt(jnp.sum(s2, axis=-1, keepdims=True) / d + LN_EPS)
    for c in range(nc):
        sl = slice(c * LANES, (c + 1) * LANES)
        y = ((u_ref[c] - mu) * rs) * lg_ref[:, sl] + lb_ref[:, sl]
        zc = z_ref[0, :, sl].astype(F32)
        o_ref[0, :, sl] = (_silu(y) * _silu(zc)).astype(BF16)


def _halo_specs(tm, d, col, r):
    nblk = tm // HALO
    last_blk = r // HALO - 1
    prev = pl.BlockSpec((1, HALO, d), lambda b, t: (b, jnp.maximum(t * nblk - 1, 0), col))
    cur = pl.BlockSpec((1, tm, d), lambda b, t: (b, t, col))
    nxt = pl.BlockSpec((1, HALO, d), lambda b, t: (b, jnp.minimum((t + 1) * nblk, last_blk), col))
    return [prev, cur, nxt]


def _conformer_mix(abz, p, s, l):
    nb, r, d3 = abz.shape
    d = d3 // 3
    tm = ROW_TILE
    assert l == tm and s % tm == 0 and d % LANES == 0
    nc = d // LANES
    nlat = s // tm
    specs = _halo_specs(tm, d, 0, r) + _halo_specs(tm, d, 1, r)
    z_spec = pl.BlockSpec((1, tm, d), lambda b, t: (b, t, 2))
    w = jnp.pad(p["dw_w"].astype(F32), ((0, 1), (0, 0))).reshape(CF_KERNEL + 1, nc, LANES).transpose(1, 0, 2)
    wb = jnp.broadcast_to(p["dw_b"].astype(F32).reshape(nc, 1, LANES), (nc, 8, LANES))
    full = lambda shape: pl.BlockSpec(shape, lambda b, t: (0,) * len(shape))
    kern = functools.partial(_conf_kernel, nlat=nlat)
    return pl.pallas_call(
        kern,
        grid=(nb, r // tm),
        in_specs=specs + [z_spec, full((nc, CF_KERNEL + 1, LANES)), full((nc, 8, LANES)),
                          full((1, d)), full((1, d))],
        out_specs=pl.BlockSpec((1, tm, d), lambda b, t: (b, t, 0)),
        out_shape=jax.ShapeDtypeStruct((nb, r, d), BF16),
        scratch_shapes=[pltpu.VMEM((nc, tm + 2 * HALO, LANES), F32), pltpu.VMEM((nc, tm, LANES), F32)],
        compiler_params=_cparams("parallel", "parallel"),
        name="conformer_mix",
    )(abz, abz, abz, abz, abz, abz, abz, w, wb, p["ln_g"].astype(F32).reshape(1, d),
      p["ln_b"].astype(F32).reshape(1, d))


def _sc_kernel(bg_ref, pc_ref, cc_ref, nc_ref, pv_ref, cv_ref, nv_ref, z_ref, w_ref, o_ref, win_ref, *, nlat):
    tm = cc_ref.shape[1]
    pvalid, nvalid = _halo_valid(nlat)

    def prod(c_ref, v_ref):
        return c_ref[0].astype(F32) * v_ref[0].astype(F32)

    win_ref[0:HALO, :] = prod(pc_ref, pv_ref) * pvalid
    win_ref[HALO:HALO + tm, :] = prod(cc_ref, cv_ref)
    win_ref[HALO + tm:2 * HALO + tm, :] = prod(nc_ref, nv_ref) * nvalid
    off = HALO - SC_KERNEL // 2
    w = w_ref[...]
    y = win_ref[pl.ds(off, tm), :] * w[0:1]
    for k in range(1, SC_KERNEL):
        y = y + win_ref[pl.ds(off + k, tm), :] * w[k:k + 1]
    o_ref[0] = ((bg_ref[0].astype(F32) * y) * _silu(z_ref[0].astype(F32))).astype(BF16)


def _shortconv_mix(bcvz, p, s, l):
    nb, r, d4 = bcvz.shape
    d = d4 // 4
    tm = ROW_TILE
    assert l == tm and s % tm == 0
    nlat = s // tm
    specs = [pl.BlockSpec((1, tm, d), lambda b, t: (b, t, 0))]
    specs += _halo_specs(tm, d, 1, r) + _halo_specs(tm, d, 2, r)
    specs.append(pl.BlockSpec((1, tm, d), lambda b, t: (b, t, 3)))
    specs.append(pl.BlockSpec((8, d), lambda b, t: (0, 0)))
    w = jnp.pad(p["conv_w"].astype(F32), ((0, 8 - SC_KERNEL), (0, 0)))
    kern = functools.partial(_sc_kernel, nlat=nlat)
    return pl.pallas_call(
        kern,
        grid=(nb, r // tm),
        in_specs=specs,
        out_specs=pl.BlockSpec((1, tm, d), lambda b, t: (b, t, 0)),
        out_shape=jax.ShapeDtypeStruct((nb, r, d), BF16),
        scratch_shapes=[pltpu.VMEM((tm + 2 * HALO, d), F32)],
        compiler_params=_cparams("parallel", "parallel"),
        name="shortconv_mix",
    )(bcvz, bcvz, bcvz, bcvz, bcvz, bcvz, bcvz, bcvz, w)


def _forward(x, c, ctx, c_ctx, layers, final_norm_g):
    nb, s, d = x.shape
    l = ctx.shape[1]
    r = s + l
    assert l == ROW_TILE and s % ROW_TILE == 0 and nb < 8
    nlat = s // ROW_TILE
    depth = len(layers)
    kinds = ("attn", "conformer", "shortconv")

    cond8 = jnp.zeros((8, d), F32).at[:nb].set(c).at[nb].set(c_ctx)
    tabs = [_mod_table(_adaln(cond8, p["ada_w"], p["ada_b"]), nb, d) for p in layers]
    rope = _rope_tables(s, l)

    xs = jnp.concatenate([x, ctx], axis=1)
    h = _prologue(xs, tabs[0], layers[0]["norm_g"], nlat)
    for i, p in enumerate(layers):
        kind = kinds[i % len(kinds)]
        last = i == depth - 1
        w_in = p["w_in"].astype(BF16)
        w_out = p["w_out"].astype(BF16)
        h2 = h.reshape(nb * r, d)
        if kind == "attn":
            qkvz = _in_proj(h2, w_in, r, rope=rope, rope_cols=w_in.shape[1] // 2).reshape(nb, r, -1)
            u = _diff_attention(qkvz, p, i, s, l, keep_ctx=not last)
        elif kind == "conformer":
            u = _conformer_mix(_in_proj(h2, w_in, r).reshape(nb, r, -1), p, s, l)
        else:
            u = _shortconv_mix(_in_proj(h2, w_in, r).reshape(nb, r, -1), p, s, l)
        if last:
            return _out_proj_final(u, w_out, xs, tabs[i], final_norm_g, s)
        xs, h = _out_proj(u, w_out, xs, tabs[i], tabs[i + 1], layers[i + 1]["norm_g"], nlat)


def kernel(x, c, ctx, c_ctx, l0_norm_g, l0_ada_w, l0_ada_b, l0_w_in, l0_lam_q1, l0_lam_k1, l0_lam_q2, l0_lam_k2, l0_head_g, l0_w_out, l1_norm_g, l1_ada_w, l1_ada_b, l1_w_in, l1_dw_w, l1_dw_b, l1_ln_g, l1_ln_b, l1_w_out, l2_norm_g, l2_ada_w, l2_ada_b, l2_w_in, l2_conv_w, l2_w_out, l3_norm_g, l3_ada_w, l3_ada_b, l3_w_in, l3_lam_q1, l3_lam_k1, l3_lam_q2, l3_lam_k2, l3_head_g, l3_w_out, final_norm_g):
    layers = [
        dict(norm_g=l0_norm_g, ada_w=l0_ada_w, ada_b=l0_ada_b, w_in=l0_w_in, lam_q1=l0_lam_q1, lam_k1=l0_lam_k1,
             lam_q2=l0_lam_q2, lam_k2=l0_lam_k2, head_g=l0_head_g, w_out=l0_w_out),
        dict(norm_g=l1_norm_g, ada_w=l1_ada_w, ada_b=l1_ada_b, w_in=l1_w_in, dw_w=l1_dw_w, dw_b=l1_dw_b,
             ln_g=l1_ln_g, ln_b=l1_ln_b, w_out=l1_w_out),
        dict(norm_g=l2_norm_g, ada_w=l2_ada_w, ada_b=l2_ada_b, w_in=l2_w_in, conv_w=l2_conv_w, w_out=l2_w_out),
        dict(norm_g=l3_norm_g, ada_w=l3_ada_w, ada_b=l3_ada_b, w_in=l3_w_in, lam_q1=l3_lam_q1, lam_k1=l3_lam_k1,
             lam_q2=l3_lam_q2, lam_k2=l3_lam_k2, head_g=l3_head_g, w_out=l3_w_out),
    ]
    return _forward(x, c, ctx, c_ctx, layers, final_norm_g)
```

```python
import functools
import math

import jax
import jax.numpy as jnp
from jax import lax
from jax.experimental import pallas as pl
from jax.experimental.pallas import tpu as pltpu

F32 = jnp.float32
BF16 = jnp.bfloat16

LANES = 128
HALO = 16
ROW_TILE = 256
DA_HEAD_DIM = 64
HEAD_W = 2 * DA_HEAD_DIM
ROPE_BASE = 10000.0
ROPE_FREQS = DA_HEAD_DIM // 4
GRID_W = 64
CF_KERNEL = 31
SC_KERNEL = 3
NORM_EPS = 1e-6
LN_EPS = 1e-5
Q_TILE = 1024
Q_SUB = 512
KV_CHUNK = 256
Q_SCALE = math.log2(math.e) / math.sqrt(DA_HEAD_DIM)
VMEM_LIMIT = 52 * 1024 * 1024


def _cparams(*sem):
    return pltpu.CompilerParams(dimension_semantics=sem, vmem_limit_bytes=VMEM_LIMIT)


def _silu(x):
    return x * jax.nn.sigmoid(x)


def _ada_kernel(c_ref, w_ref, b_ref, o_ref):
    a = _silu(c_ref[...]).astype(BF16)
    o_ref[...] = jnp.dot(a, w_ref[...].astype(BF16), preferred_element_type=F32) + b_ref[...]


def _adaln(cond8, w, b):
    d, n = w.shape
    tn = min(1024, n)
    return pl.pallas_call(
        _ada_kernel,
        grid=(n // tn,),
        in_specs=[pl.BlockSpec((8, d), lambda j: (0, 0)),
                  pl.BlockSpec((d, tn), lambda j: (0, j)),
                  pl.BlockSpec((1, tn), lambda j: (0, j))],
        out_specs=pl.BlockSpec((8, tn), lambda j: (0, j)),
        out_shape=jax.ShapeDtypeStruct((8, n), F32),
        compiler_params=_cparams("parallel"),
        name="adaln",
    )(cond8, w, b.reshape(1, n))


def _mod_table(m, nb, d):
    sh, sc, g = m[:, :d], m[:, d:2 * d], m[:, 2 * d:]
    rows = jnp.stack([1.0 + sc, sh, g], axis=1)
    lat = rows[:nb]
    ctx = jnp.broadcast_to(rows[nb][None], (nb, 3, d))
    tab = jnp.stack([lat, ctx], axis=1)
    return jnp.pad(tab, ((0, 0), (0, 0), (0, 5), (0, 0)))


def _norm_mod(x, g, scale1p, shift):
    ms = jnp.mean(x * x, axis=-1, keepdims=True)
    y = x * lax.rsqrt(ms + NORM_EPS)
    return (y * g) * scale1p + shift


def _pro_kernel(x_ref, c_ref, tab_ref, g_ref, xs_ref, h_ref, *, nlat):
    xt = jnp.where(pl.program_id(1) < nlat, x_ref[0], c_ref[0])
    xs_ref[0] = xt
    tab = tab_ref[0, 0]
    h_ref[0] = _norm_mod(xt, g_ref[...], tab[0:1], tab[1:2]).astype(BF16)


def _prologue(x, ctx, tab, norm_g):
    nb, s, d = x.shape
    l = ctx.shape[1]
    r = s + l
    nlat = s // ROW_TILE
    row = lambda b, t: (b, t, 0)
    return pl.pallas_call(
        functools.partial(_pro_kernel, nlat=nlat),
        grid=(nb, r // ROW_TILE),
        in_specs=[pl.BlockSpec((1, ROW_TILE, d), lambda b, t: (b, jnp.minimum(t, nlat - 1), 0)),
                  pl.BlockSpec((1, ROW_TILE, d), lambda b, t: (b, 0, 0)),
                  pl.BlockSpec((1, 1, 8, d), lambda b, t: (b, t // nlat, 0, 0)),
                  pl.BlockSpec((1, d), lambda b, t: (0, 0))],
        out_specs=[pl.BlockSpec((1, ROW_TILE, d), row), pl.BlockSpec((1, ROW_TILE, d), row)],
        out_shape=[jax.ShapeDtypeStruct((nb, r, d), F32), jax.ShapeDtypeStruct((nb, r, d), BF16)],
        compiler_params=_cparams("parallel", "arbitrary"),
        name="prologue",
    )(x, ctx, tab, norm_g.reshape(1, d))


def _in_kernel(h_ref, w_ref, o_ref, wb_ref):
    @pl.when(pl.program_id(1) == 0)
    def _():
        wb_ref[...] = w_ref[...].astype(BF16)

    o_ref[...] = jnp.dot(h_ref[...], wb_ref[...], preferred_element_type=F32).astype(BF16)


def _in_rope_kernel(h_ref, w_ref, cos_ref, sa_ref, sb_ref, o_ref, wb_ref, *, n_rope, tn, sub):
    j = pl.program_id(0)
    tm = h_ref.shape[0]

    @pl.when(pl.program_id(1) == 0)
    def _():
        wb_ref[...] = w_ref[...].astype(BF16)

    @pl.when(j < n_rope)
    def _():
        for r0 in range(0, tm, sub):
            rows = slice(r0, r0 + sub)
            acc = jnp.dot(h_ref[rows, :], wb_ref[...], preferred_element_type=F32)
            cos, sa, sb = cos_ref[rows, :], sa_ref[rows, :], sb_ref[rows, :]
            for c in range(tn // LANES):
                xc = acc[:, c * LANES:(c + 1) * LANES]
                rot = (xc * cos + pltpu.roll(xc, LANES - ROPE_FREQS, 1) * sa
                       + pltpu.roll(xc, ROPE_FREQS, 1) * sb)
                o_ref[rows, c * LANES:(c + 1) * LANES] = rot.astype(BF16)

    @pl.when(j >= n_rope)
    def _():
        o_ref[...] = jnp.dot(h_ref[...], wb_ref[...], preferred_element_type=F32).astype(BF16)


def _in_tiles(t, r, d, n):
    tm = r // 4
    tn = 1024 if d >= 1024 else d
    assert tm % HALO == 0 and t % tm == 0 and n % tn == 0
    return tm, tn


def _in_proj(h, w, r, rope=None, rope_cols=0):
    t, d = h.shape
    n = w.shape[1]
    tm, tn = _in_tiles(t, r, d, n)
    grid = (n // tn, t // tm)
    h_spec = pl.BlockSpec((tm, d), lambda j, i: (i, 0))
    w_spec = pl.BlockSpec((d, tn), lambda j, i: (0, j))
    o_spec = pl.BlockSpec((tm, tn), lambda j, i: (i, j))
    out_shape = jax.ShapeDtypeStruct((t, n), BF16)
    scratch = [pltpu.VMEM((d, tn), BF16)]
    if rope is None:
        return pl.pallas_call(_in_kernel, grid=grid, in_specs=[h_spec, w_spec], out_specs=o_spec,
                              out_shape=out_shape, scratch_shapes=scratch,
                              compiler_params=_cparams("parallel", "arbitrary"), name="in_proj")(h, w)
    assert rope_cols % (2 * tn) == 0 and tn % LANES == 0
    per = r // tm
    n_q = rope_cols // (2 * tn)
    t_spec = pl.BlockSpec((tm, LANES), lambda j, i: (i % per + jnp.where(j >= n_q, per, 0), 0))
    sub = tm // 4 if (tm // 4) % HALO == 0 else tm
    kern = functools.partial(_in_rope_kernel, n_rope=rope_cols // tn, tn=tn, sub=sub)
    return pl.pallas_call(kern, grid=grid, in_specs=[h_spec, w_spec, t_spec, t_spec, t_spec],
                          out_specs=o_spec, out_shape=out_shape, scratch_shapes=scratch,
                          compiler_params=_cparams("parallel", "arbitrary"),
                          name="in_proj_rope")(h, w, *rope)


def _rope_tables(s, l):
    t = jnp.arange(s)
    row = (t // GRID_W).astype(F32)
    col = (t % GRID_W).astype(F32)
    inv = ROPE_BASE ** (-jnp.arange(ROPE_FREQS, dtype=F32) / ROPE_FREQS)
    lane = jnp.arange(LANES)
    dd = lane % DA_HEAD_DIM
    axis, half, f = dd // (2 * ROPE_FREQS), (dd % (2 * ROPE_FREQS)) // ROPE_FREQS, dd % ROPE_FREQS
    pos = jnp.where(axis[None, :] == 0, row[:, None], col[:, None])
    ang = pos * inv[f][None, :]
    cos, sin = jnp.cos(ang), jnp.sin(ang)
    sa = jnp.where(half[None, :] == 0, -sin, 0.0)
    sb = jnp.where(half[None, :] == 1, sin, 0.0)
    cos = jnp.concatenate([cos, jnp.ones((l, LANES), F32)], axis=0)
    sa = jnp.concatenate([sa, jnp.zeros((l, LANES), F32)], axis=0)
    sb = jnp.concatenate([sb, jnp.zeros((l, LANES), F32)], axis=0)
    return tuple(jnp.concatenate([t * Q_SCALE, t], axis=0) for t in (cos, sa, sb))


V_ROWS = HEAD_W + HALO


SAFE_DENOM = 2.0 ** -100
BOUND_SLACK = 1.01


def _split_comps(q):
    lane = lax.broadcasted_iota(jnp.int32, q.shape, 1)
    return jnp.concatenate([jnp.where((lane // DA_HEAD_DIM) == c, q, jnp.zeros_like(q)) for c in range(2)], axis=0)


def _nt_dot(a, b):
    return lax.dot_general(a, b, (((1,), (1,)), ((), ())), preferred_element_type=F32)


def _exact_chunks(k_ref, vt_ref, q01, chunks, ck):
    def scores(j):
        return _nt_dot(k_ref[0, j * ck:(j + 1) * ck, :], q01)

    s_next = scores(chunks[0])
    m = acc = None
    for i, j in enumerate(chunks):
        s = s_next
        if i + 1 < len(chunks):
            s_next = scores(chunks[i + 1])
        mc = jnp.max(s, axis=0, keepdims=True)
        m_new = mc if m is None else jnp.maximum(m, mc)
        p = jnp.exp2(s - m_new).astype(BF16)
        pv = jnp.dot(vt_ref[:, j * ck:(j + 1) * ck], p, preferred_element_type=F32)
        acc = pv if m is None else acc * jnp.exp2(m - m_new) + pv
        m = m_new
    return acc


def _bound_chunks(k_ref, vt_ref, q01, mb, chunks, ck):
    def scores(j):
        return _nt_dot(k_ref[0, j * ck:(j + 1) * ck, :], q01)

    acc = None
    s_next = scores(chunks[0])
    for i, j in enumerate(chunks):
        s = s_next
        if i + 1 < len(chunks):
            s_next = scores(chunks[i + 1])
        p = jnp.exp2(s - mb).astype(BF16)
        pv = jnp.dot(vt_ref[:, j * ck:(j + 1) * ck], p, preferred_element_type=F32)
        acc = pv if acc is None else acc + pv
    return acc


def _attn_finish(acc, lam, hg, z, lam_init):
    t = acc.shape[1] // 2
    o = acc[0:HEAD_W] * (1.0 / acc[HEAD_W:HEAD_W + 1])
    d = o[:, 0:t] - lam * o[:, t:2 * t]
    ms = jnp.mean(d * d, axis=0, keepdims=True)
    dn = (d * lax.rsqrt(ms + NORM_EPS)).T
    on = (dn * hg) * (1.0 - lam_init)
    return (on * _silu(z.astype(F32))).astype(BF16)


def _attn_body(lam_ref, hg_ref, q_ref, qc_ref, k_ref, v_ref, z_ref, zc_ref, o_ref, vt_ref, kn_ref, *,
               lam_init, ck, sub, nlat_q, keep_ctx):
    qi = pl.program_id(2)
    nkv = k_ref.shape[1]
    nchunk = nkv // ck

    @pl.when(qi == 0)
    def _():
        vt_ref[0:HEAD_W, :] = v_ref[0].astype(F32).T.astype(BF16)
        vt_ref[HEAD_W:V_ROWS, :] = jnp.ones((HALO, nkv), BF16)
        kf = k_ref[0].astype(F32)
        r_i = lax.broadcasted_iota(jnp.int32, (HEAD_W, HEAD_W), 0) // DA_HEAD_DIM
        c_i = lax.broadcasted_iota(jnp.int32, (HEAD_W, HEAD_W), 1) // DA_HEAD_DIM
        sel = jnp.where(r_i == c_i, 1.0, 0.0).astype(BF16)
        kn2 = jnp.dot((kf * kf).astype(BF16), sel, preferred_element_type=F32)
        kn_ref[...] = jnp.broadcast_to(jnp.sqrt(jnp.max(kn2, axis=0, keepdims=True)), kn_ref.shape)

    lp = lam_ref[...]
    lam = (jnp.exp(jnp.sum(lp[0:1] * lp[1:2], axis=-1, keepdims=True))
           - jnp.exp(jnp.sum(lp[2:3] * lp[3:4], axis=-1, keepdims=True)) + lam_init)
    hg = hg_ref[...]

    @pl.when(qi < nlat_q)
    def _():
        nsub = q_ref.shape[1] // sub
        kn = kn_ref[0:1, :]
        kmax = jnp.concatenate([jnp.broadcast_to(kn[:, 0:1], (1, sub)),
                                jnp.broadcast_to(kn[:, DA_HEAD_DIM:DA_HEAD_DIM + 1], (1, sub))], axis=1)
        denoms = []
        for t in range(nsub):
            rows = slice(t * sub, (t + 1) * sub)
            q01 = _split_comps(q_ref[0, rows, :])
            qf = q01.astype(F32)
            qn2 = _nt_dot(jnp.ones((HALO, HEAD_W), BF16), (qf * qf).astype(BF16))[0:1]
            mb = jnp.sqrt(qn2) * kmax * BOUND_SLACK
            acc = _bound_chunks(k_ref, vt_ref, q01, mb, range(nchunk), ck)
            o_ref[0, rows, :] = _attn_finish(acc, lam, hg, z_ref[0, rows, :], lam_init)
            denoms.append(jnp.min(acc[HEAD_W:HEAD_W + 1]))
        safe = functools.reduce(jnp.minimum, denoms) >= SAFE_DENOM

        @pl.when(jnp.logical_not(safe))
        def _():
            def redo(t, carry):
                rows = pl.ds(pl.multiple_of(t * sub, sub), sub)
                acc2 = _exact_chunks(k_ref, vt_ref, _split_comps(q_ref[0, rows, :]), range(nchunk), ck)
                o_ref[0, rows, :] = _attn_finish(acc2, lam, hg, z_ref[0, rows, :], lam_init)
                return carry

            lax.fori_loop(0, nsub, redo, 0)

    if keep_ctx:
        @pl.when(qi == nlat_q)
        def _():
            acc = _exact_chunks(k_ref, vt_ref, _split_comps(qc_ref[0]), [nchunk - 1], ck)
            o_ref[0, 0:qc_ref.shape[1], :] = _attn_finish(acc, lam, hg, zc_ref[0], lam_init)


def _diff_attention(qkvz, p, layer_idx, s, l, keep_ctx):
    nb, r, n4 = qkvz.shape
    nh = n4 // (4 * HEAD_W)
    lam_init = 0.8 - 0.6 * math.exp(-0.3 * layer_idx)
    lamp = jnp.stack([p["lam_q1"], p["lam_k1"], p["lam_q2"], p["lam_k2"]]).astype(F32)
    lamp = jnp.pad(lamp, ((0, 4), (0, LANES - DA_HEAD_DIM)))
    hg = p["head_g"].astype(F32).reshape(1, HEAD_W)
    tq = min(Q_TILE, s)
    ck = KV_CHUNK
    sub = min(Q_SUB, tq)
    assert s % tq == 0 and tq % sub == 0 and l == ck and r % ck == 0
    nlat_q = s // tq
    last_q = nlat_q - 1
    ctx_blk = s // l
    kern = functools.partial(_attn_body, lam_init=lam_init, ck=ck, sub=sub, nlat_q=nlat_q, keep_ctx=keep_ctx)
    const = lambda b, h, i: (0, 0)
    lat = lambda col: pl.BlockSpec((1, tq, HEAD_W), lambda b, h, i: (b, jnp.minimum(i, last_q), col * nh + h))
    ctx = lambda col: pl.BlockSpec((1, l, HEAD_W), lambda b, h, i: (b, ctx_blk, col * nh + h))
    kv = lambda col: pl.BlockSpec((1, r, HEAD_W), lambda b, h, i: (b, 0, col * nh + h))
    return pl.pallas_call(
        kern,
        grid=(nb, nh, nlat_q + (1 if keep_ctx else 0)),
        in_specs=[pl.BlockSpec((8, LANES), const), pl.BlockSpec((1, LANES), const),
                  lat(0), ctx(0), kv(1), kv(2), lat(3), ctx(3)],
        out_specs=pl.BlockSpec((1, tq, HEAD_W), lambda b, h, i: (b, i, h)),
        out_shape=jax.ShapeDtypeStruct((nb, r if keep_ctx else s, nh * HEAD_W), BF16),
        scratch_shapes=[pltpu.VMEM((V_ROWS, r), BF16), pltpu.VMEM((8, LANES), F32)],
        compiler_params=_cparams("parallel", "parallel", "arbitrary"),
        name="diff_attn",
    )(lamp, hg, qkvz, qkvz, qkvz, qkvz, qkvz, qkvz)


OUT_SUB = ROW_TILE


def _out_kernel(u_ref, w_ref, x_ref, tg_ref, tn_ref, g_ref, xo_ref, ho_ref):
    gate = tg_ref[0, 0][2:3]
    tab = tn_ref[0, 0]
    for r0 in range(0, u_ref.shape[1], OUT_SUB):
        rows = slice(r0, r0 + OUT_SUB)
        y = jnp.dot(u_ref[0, rows, :], w_ref[...], preferred_element_type=F32)
        xn = x_ref[0, rows, :] + gate * y
        xo_ref[0, rows, :] = xn
        ho_ref[0, rows, :] = _norm_mod(xn, g_ref[...], tab[0:1], tab[1:2]).astype(BF16)


def _out_final_kernel(u_ref, w_ref, x_ref, tg_ref, g_ref, o_ref):
    gate = tg_ref[0, 0][2:3]
    for r0 in range(0, u_ref.shape[1], OUT_SUB):
        rows = slice(r0, r0 + OUT_SUB)
        y = jnp.dot(u_ref[0, rows, :], w_ref[...], preferred_element_type=F32)
        xn = x_ref[0, rows, :] + gate * y
        ms = jnp.mean(xn * xn, axis=-1, keepdims=True)
        o_ref[0, rows, :] = (xn * lax.rsqrt(ms + NORM_EPS)) * g_ref[...]


def _out_proj(u, w, x, tab, tab_next, norm_g_next, nlat):
    nb, r, d = x.shape
    dw = u.shape[2]
    row = lambda b, t: (b, t, 0)
    tabspec = pl.BlockSpec((1, 1, 8, d), lambda b, t: (b, t // nlat, 0, 0))
    return pl.pallas_call(
        _out_kernel,
        grid=(nb, r // ROW_TILE),
        in_specs=[pl.BlockSpec((1, ROW_TILE, dw), row),
                  pl.BlockSpec((dw, d), lambda b, t: (0, 0)),
                  pl.BlockSpec((1, ROW_TILE, d), row),
                  tabspec, tabspec,
                  pl.BlockSpec((1, d), lambda b, t: (0, 0))],
        out_specs=[pl.BlockSpec((1, ROW_TILE, d), row), pl.BlockSpec((1, ROW_TILE, d), row)],
        out_shape=[jax.ShapeDtypeStruct((nb, r, d), F32), jax.ShapeDtypeStruct((nb, r, d), BF16)],
        compiler_params=_cparams("parallel", "parallel"),
        name="out_proj",
    )(u, w, x, tab, tab_next, norm_g_next.reshape(1, d))


def _out_proj_final(u, w, x, tab, final_g, s):
    nb, r, d = x.shape
    dw = u.shape[2]
    row = lambda b, t: (b, t, 0)
    return pl.pallas_call(
        _out_final_kernel,
        grid=(nb, s // ROW_TILE),
        in_specs=[pl.BlockSpec((1, ROW_TILE, dw), row),
                  pl.BlockSpec((dw, d), lambda b, t: (0, 0)),
                  pl.BlockSpec((1, ROW_TILE, d), row),
                  pl.BlockSpec((1, 1, 8, d), lambda b, t: (b, 0, 0, 0)),
                  pl.BlockSpec((1, d), lambda b, t: (0, 0))],
        out_specs=pl.BlockSpec((1, ROW_TILE, d), row),
        out_shape=jax.ShapeDtypeStruct((nb, s, d), F32),
        compiler_params=_cparams("parallel", "parallel"),
        name="out_proj_final",
    )(u, w, x, tab, final_g.reshape(1, d))


def _halo_valid(nlat):
    t = pl.program_id(1)
    prev_ok = jnp.logical_and(t != 0, t != nlat)
    next_ok = jnp.logical_and(t != nlat - 1, t != nlat)
    return prev_ok.astype(F32), next_ok.astype(F32)


def _conf_kernel(pa_ref, ca_ref, na_ref, pb_ref, cb_ref, nb_ref, z_ref, w_ref, wb_ref, lg_ref, lb_ref,
                 o_ref, win_ref, u_ref, *, nlat):
    tm = ca_ref.shape[1]
    nc = win_ref.shape[0]
    pv, nv = _halo_valid(nlat)

    def glu(a_ref, b_ref):
        return a_ref[0].astype(F32) * jax.nn.sigmoid(b_ref[0].astype(F32))

    gp, gc, gn = glu(pa_ref, pb_ref) * pv, glu(ca_ref, cb_ref), glu(na_ref, nb_ref) * nv
    for c in range(nc):
        sl = slice(c * LANES, (c + 1) * LANES)
        win_ref[c, 0:HALO, :] = gp[:, sl]
        win_ref[c, HALO:HALO + tm, :] = gc[:, sl]
        win_ref[c, HALO + tm:2 * HALO + tm, :] = gn[:, sl]

    off = HALO - CF_KERNEL // 2

    def conv_chunk(c, carry):
        w = w_ref[c]
        acc = jnp.zeros((tm, LANES), F32)
        for k in range(CF_KERNEL):
            acc = acc + win_ref[c, pl.ds(off + k, tm), :] * w[k:k + 1, :]
        u_ref[c] = acc + wb_ref[c][0:1]
        return carry

    lax.fori_loop(0, nc, conv_chunk, 0)

    d = nc * LANES
    s1 = u_ref[0]
    for c in range(1, nc):
        s1 = s1 + u_ref[c]
    mu = jnp.sum(s1, axis=-1, keepdims=True) / d
    s2 = jnp.square(u_ref[0] - mu)
    for c in range(1, nc):
        s2 = s2 + jnp.square(u_ref[c] - mu)
    rs = lax.rsqrt(jnp.sum(s2, axis=-1, keepdims=True) / d + LN_EPS)
    for c in range(nc):
        sl = slice(c * LANES, (c + 1) * LANES)
        y = ((u_ref[c] - mu) * rs) * lg_ref[:, sl] + lb_ref[:, sl]
        zc = z_ref[0, :, sl].astype(F32)
        o_ref[0, :, sl] = (_silu(y) * _silu(zc)).astype(BF16)


def _halo_specs(tm, d, col, r):
    nblk = tm // HALO
    last_blk = r // HALO - 1
    prev = pl.BlockSpec((1, HALO, d), lambda b, t: (b, jnp.maximum(t * nblk - 1, 0), col))
    cur = pl.BlockSpec((1, tm, d), lambda b, t: (b, t, col))
    nxt = pl.BlockSpec((1, HALO, d), lambda b, t: (b, jnp.minimum((t + 1) * nblk, last_blk), col))
    return [prev, cur, nxt]


def _conformer_mix(abz, p, s, l):
    nb, r, d3 = abz.shape
    d = d3 // 3
    tm = ROW_TILE
    assert l == tm and s % tm == 0 and d % LANES == 0
    nc = d // LANES
    nlat = s // tm
    specs = _halo_specs(tm, d, 0, r) + _halo_specs(tm, d, 1, r)
    z_spec = pl.BlockSpec((1, tm, d), lambda b, t: (b, t, 2))
    w = jnp.pad(p["dw_w"].astype(F32), ((0, 1), (0, 0))).reshape(CF_KERNEL + 1, nc, LANES).transpose(1, 0, 2)
    wb = jnp.broadcast_to(p["dw_b"].astype(F32).reshape(nc, 1, LANES), (nc, 8, LANES))
    full = lambda shape: pl.BlockSpec(shape, lambda b, t: (0,) * len(shape))
    kern = functools.partial(_conf_kernel, nlat=nlat)
    return pl.pallas_call(
        kern,
        grid=(nb, r // tm),
        in_specs=specs + [z_spec, full((nc, CF_KERNEL + 1, LANES)), full((nc, 8, LANES)),
                          full((1, d)), full((1, d))],
        out_specs=pl.BlockSpec((1, tm, d), lambda b, t: (b, t, 0)),
        out_shape=jax.ShapeDtypeStruct((nb, r, d), BF16),
        scratch_shapes=[pltpu.VMEM((nc, tm + 2 * HALO, LANES), F32), pltpu.VMEM((nc, tm, LANES), F32)],
        compiler_params=_cparams("parallel", "parallel"),
        name="conformer_mix",
    )(abz, abz, abz, abz, abz, abz, abz, w, wb, p["ln_g"].astype(F32).reshape(1, d),
      p["ln_b"].astype(F32).reshape(1, d))


def _sc_kernel(bg_ref, pc_ref, cc_ref, nc_ref, pv_ref, cv_ref, nv_ref, z_ref, w_ref, o_ref, win_ref, *, nlat):
    tm = cc_ref.shape[1]
    pvalid, nvalid = _halo_valid(nlat)

    def prod(c_ref, v_ref):
        return c_ref[0].astype(F32) * v_ref[0].astype(F32)

    win_ref[0:HALO, :] = prod(pc_ref, pv_ref) * pvalid
    win_ref[HALO:HALO + tm, :] = prod(cc_ref, cv_ref)
    win_ref[HALO + tm:2 * HALO + tm, :] = prod(nc_ref, nv_ref) * nvalid
    off = HALO - SC_KERNEL // 2
    w = w_ref[...]
    y = win_ref[pl.ds(off, tm), :] * w[0:1]
    for k in range(1, SC_KERNEL):
        y = y + win_ref[pl.ds(off + k, tm), :] * w[k:k + 1]
    o_ref[0] = ((bg_ref[0].astype(F32) * y) * _silu(z_ref[0].astype(F32))).astype(BF16)


def _shortconv_mix(bcvz, p, s, l):
    nb, r, d4 = bcvz.shape
    d = d4 // 4
    tm = ROW_TILE
    assert l == tm and s % tm == 0
    nlat = s // tm
    specs = [pl.BlockSpec((1, tm, d), lambda b, t: (b, t, 0))]
    specs += _halo_specs(tm, d, 1, r) + _halo_specs(tm, d, 2, r)
    specs.append(pl.BlockSpec((1, tm, d), lambda b, t: (b, t, 3)))
    specs.append(pl.BlockSpec((8, d), lambda b, t: (0, 0)))
    w = jnp.pad(p["conv_w"].astype(F32), ((0, 8 - SC_KERNEL), (0, 0)))
    kern = functools.partial(_sc_kernel, nlat=nlat)
    return pl.pallas_call(
        kern,
        grid=(nb, r // tm),
        in_specs=specs,
        out_specs=pl.BlockSpec((1, tm, d), lambda b, t: (b, t, 0)),
        out_shape=jax.ShapeDtypeStruct((nb, r, d), BF16),
        scratch_shapes=[pltpu.VMEM((tm + 2 * HALO, d), F32)],
        compiler_params=_cparams("parallel", "parallel"),
        name="shortconv_mix",
    )(bcvz, bcvz, bcvz, bcvz, bcvz, bcvz, bcvz, bcvz, w)


def _forward(x, c, ctx, c_ctx, layers, final_norm_g):
    nb, s, d = x.shape
    l = ctx.shape[1]
    r = s + l
    assert l == ROW_TILE and s % ROW_TILE == 0 and nb < 8
    nlat = s // ROW_TILE
    depth = len(layers)
    kinds = ("attn", "conformer", "shortconv")

    cond8 = jnp.zeros((8, d), F32).at[:nb].set(c).at[nb].set(c_ctx)
    tabs = [_mod_table(_adaln(cond8, p["ada_w"], p["ada_b"]), nb, d) for p in layers]
    rope = _rope_tables(s, l)

    xs, h = _prologue(x, ctx, tabs[0], layers[0]["norm_g"])
    for i, p in enumerate(layers):
        kind = kinds[i % len(kinds)]
        last = i == depth - 1
        w_in = p["w_in"]
        w_out = p["w_out"].astype(BF16)
        h2 = h.reshape(nb * r, d)
        if kind == "attn":
            qkvz = _in_proj(h2, w_in, r, rope=rope, rope_cols=w_in.shape[1] // 2).reshape(nb, r, -1)
            u = _diff_attention(qkvz, p, i, s, l, keep_ctx=not last)
        elif kind == "conformer":
            u = _conformer_mix(_in_proj(h2, w_in, r).reshape(nb, r, -1), p, s, l)
        else:
            u = _shortconv_mix(_in_proj(h2, w_in, r).reshape(nb, r, -1), p, s, l)
        if last:
            return _out_proj_final(u, w_out, xs, tabs[i], final_norm_g, s)
        xs, h = _out_proj(u, w_out, xs, tabs[i], tabs[i + 1], layers[i + 1]["norm_g"], nlat)


def kernel(x, c, ctx, c_ctx, l0_norm_g, l0_ada_w, l0_ada_b, l0_w_in, l0_lam_q1, l0_lam_k1, l0_lam_q2, l0_lam_k2, l0_head_g, l0_w_out, l1_norm_g, l1_ada_w, l1_ada_b, l1_w_in, l1_dw_w, l1_dw_b, l1_ln_g, l1_ln_b, l1_w_out, l2_norm_g, l2_ada_w, l2_ada_b, l2_w_in, l2_conv_w, l2_w_out, l3_norm_g, l3_ada_w, l3_ada_b, l3_w_in, l3_lam_q1, l3_lam_k1, l3_lam_q2, l3_lam_k2, l3_head_g, l3_w_out, final_norm_g):
    layers = [
        dict(norm_g=l0_norm_g, ada_w=l0_ada_w, ada_b=l0_ada_b, w_in=l0_w_in, lam_q1=l0_lam_q1, lam_k1=l0_lam_k1,
             lam_q2=l0_lam_q2, lam_k2=l0_lam_k2, head_g=l0_head_g, w_out=l0_w_out),
        dict(norm_g=l1_norm_g, ada_w=l1_ada_w, ada_b=l1_ada_b, w_in=l1_w_in, dw_w=l1_dw_w, dw_b=l1_dw_b,
             ln_g=l1_ln_g, ln_b=l1_ln_b, w_out=l1_w_out),
        dict(norm_g=l2_norm_g, ada_w=l2_ada_w, ada_b=l2_ada_b, w_in=l2_w_in, conv_w=l2_conv_w, w_out=l2_w_out),
        dict(norm_g=l3_norm_g, ada_w=l3_ada_w, ada_b=l3_ada_b, w_in=l3_w_in, lam_q1=l3_lam_q1, lam_k1=l3_lam_k1,
             lam_q2=l3_lam_q2, lam_k2=l3_lam_k2, head_g=l3_head_g, w_out=l3_w_out),
    ]
    return _forward(x, c, ctx, c_ctx, layers, final_norm_g)
```

```python
import functools
import math

import jax
import jax.numpy as jnp
from jax import lax
from jax.experimental import pallas as pl
from jax.experimental.pallas import tpu as pltpu

F32 = jnp.float32
BF16 = jnp.bfloat16

LANES = 128
HALO = 16
ROW_TILE = 256
DA_HEAD_DIM = 64
HEAD_W = 2 * DA_HEAD_DIM
ROPE_BASE = 10000.0
ROPE_FREQS = DA_HEAD_DIM // 4
GRID_W = 64
CF_KERNEL = 31
SC_KERNEL = 3
NORM_EPS = 1e-6
LN_EPS = 1e-5
Q_TILE = 2048
Q_SUB = 512
FINAL_TILE = 512
KV_CHUNK = 256
Q_SCALE = math.log2(math.e) / math.sqrt(DA_HEAD_DIM)
VMEM_LIMIT = 52 * 1024 * 1024


def _cparams(*sem):
    return pltpu.CompilerParams(dimension_semantics=sem, vmem_limit_bytes=VMEM_LIMIT)


def _silu(x):
    return x * jax.nn.sigmoid(x)


def _ada_kernel(c_ref, w_ref, b_ref, o_ref):
    a = _silu(c_ref[...]).astype(BF16)
    o_ref[...] = jnp.dot(a, w_ref[...].astype(BF16), preferred_element_type=F32) + b_ref[...]


def _adaln(cond8, w, b):
    d, n = w.shape
    tn = min(1024, n)
    return pl.pallas_call(
        _ada_kernel,
        grid=(n // tn,),
        in_specs=[pl.BlockSpec((8, d), lambda j: (0, 0)),
                  pl.BlockSpec((d, tn), lambda j: (0, j)),
                  pl.BlockSpec((1, tn), lambda j: (0, j))],
        out_specs=pl.BlockSpec((8, tn), lambda j: (0, j)),
        out_shape=jax.ShapeDtypeStruct((8, n), F32),
        compiler_params=_cparams("parallel"),
        name="adaln",
    )(cond8, w, b.reshape(1, n))


def _mod_table(m, nb, d):
    sh, sc, g = m[:, :d], m[:, d:2 * d], m[:, 2 * d:]
    rows = jnp.stack([1.0 + sc, sh, g], axis=1)
    lat = rows[:nb]
    ctx = jnp.broadcast_to(rows[nb][None], (nb, 3, d))
    tab = jnp.stack([lat, ctx], axis=1)
    return jnp.pad(tab, ((0, 0), (0, 0), (0, 5), (0, 0)))


def _norm_mod(x, g, scale1p, shift):
    ms = jnp.mean(x * x, axis=-1, keepdims=True)
    y = x * lax.rsqrt(ms + NORM_EPS)
    return (y * g) * scale1p + shift


def _pro_kernel(x_ref, c_ref, tab_ref, g_ref, xs_ref, h_ref, *, nlat):
    xt = jnp.where(pl.program_id(1) < nlat, x_ref[0], c_ref[0])
    xs_ref[0] = xt
    tab = tab_ref[0, 0]
    h_ref[0] = _norm_mod(xt, g_ref[...], tab[0:1], tab[1:2]).astype(BF16)


def _prologue(x, ctx, tab, norm_g):
    nb, s, d = x.shape
    l = ctx.shape[1]
    r = s + l
    nlat = s // ROW_TILE
    row = lambda b, t: (b, t, 0)
    return pl.pallas_call(
        functools.partial(_pro_kernel, nlat=nlat),
        grid=(nb, r // ROW_TILE),
        in_specs=[pl.BlockSpec((1, ROW_TILE, d), lambda b, t: (b, jnp.minimum(t, nlat - 1), 0)),
                  pl.BlockSpec((1, ROW_TILE, d), lambda b, t: (b, 0, 0)),
                  pl.BlockSpec((1, 1, 8, d), lambda b, t: (b, t // nlat, 0, 0)),
                  pl.BlockSpec((1, d), lambda b, t: (0, 0))],
        out_specs=[pl.BlockSpec((1, ROW_TILE, d), row), pl.BlockSpec((1, ROW_TILE, d), row)],
        out_shape=[jax.ShapeDtypeStruct((nb, r, d), F32), jax.ShapeDtypeStruct((nb, r, d), BF16)],
        compiler_params=_cparams("parallel", "arbitrary"),
        name="prologue",
    )(x, ctx, tab, norm_g.reshape(1, d))


def _in_kernel(h_ref, w_ref, o_ref, wb_ref):
    @pl.when(pl.program_id(1) == 0)
    def _():
        wb_ref[...] = w_ref[...].astype(BF16)

    o_ref[...] = jnp.dot(h_ref[...], wb_ref[...], preferred_element_type=F32).astype(BF16)


def _in_rope_kernel(h_ref, w_ref, cos_ref, sa_ref, sb_ref, o_ref, wb_ref, *, n_rope, tn, sub):
    j = pl.program_id(0)
    tm = h_ref.shape[0]

    @pl.when(pl.program_id(1) == 0)
    def _():
        wb_ref[...] = w_ref[...].astype(BF16)

    @pl.when(j < n_rope)
    def _():
        for r0 in range(0, tm, sub):
            rows = slice(r0, r0 + sub)
            acc = jnp.dot(h_ref[rows, :], wb_ref[...], preferred_element_type=F32)
            cos, sa, sb = cos_ref[rows, :], sa_ref[rows, :], sb_ref[rows, :]
            for c in range(tn // LANES):
                xc = acc[:, c * LANES:(c + 1) * LANES]
                rot = (xc * cos + pltpu.roll(xc, LANES - ROPE_FREQS, 1) * sa
                       + pltpu.roll(xc, ROPE_FREQS, 1) * sb)
                o_ref[rows, c * LANES:(c + 1) * LANES] = rot.astype(BF16)

    @pl.when(j >= n_rope)
    def _():
        o_ref[...] = jnp.dot(h_ref[...], wb_ref[...], preferred_element_type=F32).astype(BF16)


def _in_tiles(t, r, d, n):
    tm = r // 4
    tn = 1024 if d >= 1024 else d
    assert tm % HALO == 0 and t % tm == 0 and n % tn == 0
    return tm, tn


def _in_proj(h, w, r, rope=None, rope_cols=0):
    t, d = h.shape
    n = w.shape[1]
    tm, tn = _in_tiles(t, r, d, n)
    grid = (n // tn, t // tm)
    h_spec = pl.BlockSpec((tm, d), lambda j, i: (i, 0))
    w_spec = pl.BlockSpec((d, tn), lambda j, i: (0, j))
    o_spec = pl.BlockSpec((tm, tn), lambda j, i: (i, j))
    out_shape = jax.ShapeDtypeStruct((t, n), BF16)
    scratch = [pltpu.VMEM((d, tn), BF16)]
    if rope is None:
        return pl.pallas_call(_in_kernel, grid=grid, in_specs=[h_spec, w_spec], out_specs=o_spec,
                              out_shape=out_shape, scratch_shapes=scratch,
                              compiler_params=_cparams("parallel", "arbitrary"), name="in_proj")(h, w)
    assert rope_cols % (2 * tn) == 0 and tn % LANES == 0
    per = r // tm
    n_q = rope_cols // (2 * tn)
    t_spec = pl.BlockSpec((tm, LANES), lambda j, i: (i % per + jnp.where(j >= n_q, per, 0), 0))
    sub = tm // 4 if (tm // 4) % HALO == 0 else tm
    kern = functools.partial(_in_rope_kernel, n_rope=rope_cols // tn, tn=tn, sub=sub)
    return pl.pallas_call(kern, grid=grid, in_specs=[h_spec, w_spec, t_spec, t_spec, t_spec],
                          out_specs=o_spec, out_shape=out_shape, scratch_shapes=scratch,
                          compiler_params=_cparams("parallel", "arbitrary"),
                          name="in_proj_rope")(h, w, *rope)


def _rope_tables(s, l):
    t = jnp.arange(s)
    row = (t // GRID_W).astype(F32)
    col = (t % GRID_W).astype(F32)
    inv = ROPE_BASE ** (-jnp.arange(ROPE_FREQS, dtype=F32) / ROPE_FREQS)
    lane = jnp.arange(LANES)
    dd = lane % DA_HEAD_DIM
    axis, half, f = dd // (2 * ROPE_FREQS), (dd % (2 * ROPE_FREQS)) // ROPE_FREQS, dd % ROPE_FREQS
    pos = jnp.where(axis[None, :] == 0, row[:, None], col[:, None])
    ang = pos * inv[f][None, :]
    cos, sin = jnp.cos(ang), jnp.sin(ang)
    sa = jnp.where(half[None, :] == 0, -sin, 0.0)
    sb = jnp.where(half[None, :] == 1, sin, 0.0)
    cos = jnp.concatenate([cos, jnp.ones((l, LANES), F32)], axis=0)
    sa = jnp.concatenate([sa, jnp.zeros((l, LANES), F32)], axis=0)
    sb = jnp.concatenate([sb, jnp.zeros((l, LANES), F32)], axis=0)
    return tuple(jnp.concatenate([t * Q_SCALE, t], axis=0) for t in (cos, sa, sb))


V_ROWS = HEAD_W + HALO


SAFE_DENOM = 2.0 ** -100
BOUND_SLACK = 1.01


def _split_comps(q):
    lane = lax.broadcasted_iota(jnp.int32, q.shape, 1)
    return jnp.concatenate([jnp.where((lane // DA_HEAD_DIM) == c, q, jnp.zeros_like(q)) for c in range(2)], axis=0)


def _nt_dot(a, b):
    return lax.dot_general(a, b, (((1,), (1,)), ((), ())), preferred_element_type=F32)


def _exact_chunks(k_ref, vt_ref, q01, chunks, ck):
    def scores(j):
        return _nt_dot(k_ref[0, j * ck:(j + 1) * ck, :], q01)

    s_next = scores(chunks[0])
    m = acc = None
    for i, j in enumerate(chunks):
        s = s_next
        if i + 1 < len(chunks):
            s_next = scores(chunks[i + 1])
        mc = jnp.max(s, axis=0, keepdims=True)
        m_new = mc if m is None else jnp.maximum(m, mc)
        p = jnp.exp2(s - m_new).astype(BF16)
        pv = jnp.dot(vt_ref[:, j * ck:(j + 1) * ck], p, preferred_element_type=F32)
        acc = pv if m is None else acc * jnp.exp2(m - m_new) + pv
        m = m_new
    return acc


def _bound_chunks(k_ref, vt_ref, q01, mb, chunks, ck):
    def scores(j):
        return _nt_dot(k_ref[0, j * ck:(j + 1) * ck, :], q01)

    acc = None
    s_next = scores(chunks[0])
    for i, j in enumerate(chunks):
        s = s_next
        if i + 1 < len(chunks):
            s_next = scores(chunks[i + 1])
        p = jnp.exp2(s - mb).astype(BF16)
        pv = jnp.dot(vt_ref[:, j * ck:(j + 1) * ck], p, preferred_element_type=F32)
        acc = pv if acc is None else acc + pv
    return acc


def _attn_finish(acc, lam, hg, z, lam_init):
    t = acc.shape[1] // 2
    o = acc[0:HEAD_W] * (1.0 / acc[HEAD_W:HEAD_W + 1])
    d = o[:, 0:t] - lam * o[:, t:2 * t]
    ms = jnp.mean(d * d, axis=0, keepdims=True)
    dn = (d * lax.rsqrt(ms + NORM_EPS)).T
    on = (dn * hg) * (1.0 - lam_init)
    return (on * _silu(z.astype(F32))).astype(BF16)


def _attn_body(lam_ref, hg_ref, q_ref, qc_ref, k_ref, v_ref, z_ref, zc_ref, o_ref, vt_ref, kn_ref, *,
               lam_init, ck, sub, nlat_q, keep_ctx):
    qi = pl.program_id(2)
    nkv = k_ref.shape[1]
    nchunk = nkv // ck

    @pl.when(qi == 0)
    def _():
        vt_ref[0:HEAD_W, :] = v_ref[0].astype(F32).T.astype(BF16)
        vt_ref[HEAD_W:V_ROWS, :] = jnp.ones((HALO, nkv), BF16)
        kf = k_ref[0].astype(F32)
        r_i = lax.broadcasted_iota(jnp.int32, (HEAD_W, HEAD_W), 0) // DA_HEAD_DIM
        c_i = lax.broadcasted_iota(jnp.int32, (HEAD_W, HEAD_W), 1) // DA_HEAD_DIM
        sel = jnp.where(r_i == c_i, 1.0, 0.0).astype(BF16)
        kn2 = jnp.dot((kf * kf).astype(BF16), sel, preferred_element_type=F32)
        kn_ref[...] = jnp.broadcast_to(jnp.sqrt(jnp.max(kn2, axis=0, keepdims=True)), kn_ref.shape)

    lp = lam_ref[...]
    lam = (jnp.exp(jnp.sum(lp[0:1] * lp[1:2], axis=-1, keepdims=True))
           - jnp.exp(jnp.sum(lp[2:3] * lp[3:4], axis=-1, keepdims=True)) + lam_init)
    hg = hg_ref[...]

    @pl.when(qi < nlat_q)
    def _():
        nsub = q_ref.shape[1] // sub
        kn = kn_ref[0:1, :]
        kmax = jnp.concatenate([jnp.broadcast_to(kn[:, 0:1], (1, sub)),
                                jnp.broadcast_to(kn[:, DA_HEAD_DIM:DA_HEAD_DIM + 1], (1, sub))], axis=1)
        denoms = []
        for t in range(nsub):
            rows = slice(t * sub, (t + 1) * sub)
            q01 = _split_comps(q_ref[0, rows, :])
            qf = q01.astype(F32)
            qn2 = _nt_dot(jnp.ones((HALO, HEAD_W), BF16), (qf * qf).astype(BF16))[0:1]
            mb = jnp.sqrt(qn2) * kmax * BOUND_SLACK
            acc = _bound_chunks(k_ref, vt_ref, q01, mb, range(nchunk), ck)
            o_ref[0, rows, :] = _attn_finish(acc, lam, hg, z_ref[0, rows, :], lam_init)
            denoms.append(jnp.min(acc[HEAD_W:HEAD_W + 1]))
        safe = functools.reduce(jnp.minimum, denoms) >= SAFE_DENOM

        @pl.when(jnp.logical_not(safe))
        def _():
            def redo(t, carry):
                rows = pl.ds(pl.multiple_of(t * sub, sub), sub)
                acc2 = _exact_chunks(k_ref, vt_ref, _split_comps(q_ref[0, rows, :]), range(nchunk), ck)
                o_ref[0, rows, :] = _attn_finish(acc2, lam, hg, z_ref[0, rows, :], lam_init)
                return carry

            lax.fori_loop(0, nsub, redo, 0)

    if keep_ctx:
        @pl.when(qi == nlat_q)
        def _():
            acc = _exact_chunks(k_ref, vt_ref, _split_comps(qc_ref[0]), [nchunk - 1], ck)
            o_ref[0, 0:qc_ref.shape[1], :] = _attn_finish(acc, lam, hg, zc_ref[0], lam_init)


def _diff_attention(qkvz, p, layer_idx, s, l, keep_ctx):
    nb, r, n4 = qkvz.shape
    nh = n4 // (4 * HEAD_W)
    lam_init = 0.8 - 0.6 * math.exp(-0.3 * layer_idx)
    lamp = jnp.stack([p["lam_q1"], p["lam_k1"], p["lam_q2"], p["lam_k2"]]).astype(F32)
    lamp = jnp.pad(lamp, ((0, 4), (0, LANES - DA_HEAD_DIM)))
    hg = p["head_g"].astype(F32).reshape(1, HEAD_W)
    tq = min(Q_TILE, s)
    ck = KV_CHUNK
    sub = min(Q_SUB, tq)
    assert s % tq == 0 and tq % sub == 0 and l == ck and r % ck == 0
    nlat_q = s // tq
    last_q = nlat_q - 1
    ctx_blk = s // l
    kern = functools.partial(_attn_body, lam_init=lam_init, ck=ck, sub=sub, nlat_q=nlat_q, keep_ctx=keep_ctx)
    const = lambda b, h, i: (0, 0)
    lat = lambda col: pl.BlockSpec((1, tq, HEAD_W), lambda b, h, i: (b, jnp.minimum(i, last_q), col * nh + h))
    ctx = lambda col: pl.BlockSpec((1, l, HEAD_W), lambda b, h, i: (b, ctx_blk, col * nh + h))
    kv = lambda col: pl.BlockSpec((1, r, HEAD_W), lambda b, h, i: (b, 0, col * nh + h))
    return pl.pallas_call(
        kern,
        grid=(nb, nh, nlat_q + (1 if keep_ctx else 0)),
        in_specs=[pl.BlockSpec((8, LANES), const), pl.BlockSpec((1, LANES), const),
                  lat(0), ctx(0), kv(1), kv(2), lat(3), ctx(3)],
        out_specs=pl.BlockSpec((1, tq, HEAD_W), lambda b, h, i: (b, i, h)),
        out_shape=jax.ShapeDtypeStruct((nb, r if keep_ctx else s, nh * HEAD_W), BF16),
        scratch_shapes=[pltpu.VMEM((V_ROWS, r), BF16), pltpu.VMEM((8, LANES), F32)],
        compiler_params=_cparams("parallel", "parallel", "arbitrary"),
        name="diff_attn",
    )(lamp, hg, qkvz, qkvz, qkvz, qkvz, qkvz, qkvz)


def _residual_out(u, w_ref, x_ref, tg_ref, tn_ref, g_ref, xo_ref, ho_ref):
    y = jnp.dot(u, w_ref[...], preferred_element_type=F32)
    xn = x_ref[0] + tg_ref[0, 0][2:3] * y
    xo_ref[0] = xn
    tab = tn_ref[0, 0]
    ho_ref[0] = _norm_mod(xn, g_ref[...], tab[0:1], tab[1:2]).astype(BF16)


def _out_kernel(u_ref, w_ref, x_ref, tg_ref, tn_ref, g_ref, xo_ref, ho_ref):
    _residual_out(u_ref[0], w_ref, x_ref, tg_ref, tn_ref, g_ref, xo_ref, ho_ref)


def _out_final_kernel(u_ref, w_ref, x_ref, tg_ref, g_ref, o_ref):
    y = jnp.dot(u_ref[0], w_ref[...], preferred_element_type=F32)
    xn = x_ref[0] + tg_ref[0, 0][2:3] * y
    ms = jnp.mean(xn * xn, axis=-1, keepdims=True)
    o_ref[0] = (xn * lax.rsqrt(ms + NORM_EPS)) * g_ref[...]


def _out_proj(u, w, x, tab, tab_next, norm_g_next, nlat):
    nb, r, d = x.shape
    dw = u.shape[2]
    row = lambda b, t: (b, t, 0)
    tabspec = pl.BlockSpec((1, 1, 8, d), lambda b, t: (b, t // nlat, 0, 0))
    return pl.pallas_call(
        _out_kernel,
        grid=(nb, r // ROW_TILE),
        in_specs=[pl.BlockSpec((1, ROW_TILE, dw), row),
                  pl.BlockSpec((dw, d), lambda b, t: (0, 0)),
                  pl.BlockSpec((1, ROW_TILE, d), row),
                  tabspec, tabspec,
                  pl.BlockSpec((1, d), lambda b, t: (0, 0))],
        out_specs=[pl.BlockSpec((1, ROW_TILE, d), row), pl.BlockSpec((1, ROW_TILE, d), row)],
        out_shape=[jax.ShapeDtypeStruct((nb, r, d), F32), jax.ShapeDtypeStruct((nb, r, d), BF16)],
        compiler_params=_cparams("parallel", "parallel"),
        name="out_proj",
    )(u, w, x, tab, tab_next, norm_g_next.reshape(1, d))


def _out_proj_final(u, w, x, tab, final_g, s):
    nb, r, d = x.shape
    dw = u.shape[2]
    row = lambda b, t: (b, t, 0)
    tm = FINAL_TILE if s % FINAL_TILE == 0 else ROW_TILE
    return pl.pallas_call(
        _out_final_kernel,
        grid=(nb, s // tm),
        in_specs=[pl.BlockSpec((1, tm, dw), row),
                  pl.BlockSpec((dw, d), lambda b, t: (0, 0)),
                  pl.BlockSpec((1, tm, d), row),
                  pl.BlockSpec((1, 1, 8, d), lambda b, t: (b, 0, 0, 0)),
                  pl.BlockSpec((1, d), lambda b, t: (0, 0))],
        out_specs=pl.BlockSpec((1, tm, d), row),
        out_shape=jax.ShapeDtypeStruct((nb, s, d), F32),
        compiler_params=_cparams("parallel", "parallel"),
        name="out_proj_final",
    )(u, w, x, tab, final_g.reshape(1, d))


def _halo_valid(nlat):
    t = pl.program_id(1)
    prev_ok = jnp.logical_and(t != 0, t != nlat)
    next_ok = jnp.logical_and(t != nlat - 1, t != nlat)
    return prev_ok.astype(F32), next_ok.astype(F32)


def _conf_out_kernel(pa_ref, ca_ref, na_ref, pb_ref, cb_ref, nb_ref, z_ref, w_ref, wb_ref, lg_ref, lb_ref,
                     wo_ref, x_ref, tg_ref, tn_ref, g_ref, xo_ref, ho_ref, win_ref, u_ref, *, nlat):
    tm = ca_ref.shape[1]
    nc = win_ref.shape[0]
    pv, nv = _halo_valid(nlat)

    def glu(a_ref, b_ref):
        return a_ref[0].astype(F32) * jax.nn.sigmoid(b_ref[0].astype(F32))

    gp, gc, gn = glu(pa_ref, pb_ref) * pv, glu(ca_ref, cb_ref), glu(na_ref, nb_ref) * nv
    for c in range(nc):
        sl = slice(c * LANES, (c + 1) * LANES)
        win_ref[c, 0:HALO, :] = gp[:, sl]
        win_ref[c, HALO:HALO + tm, :] = gc[:, sl]
        win_ref[c, HALO + tm:2 * HALO + tm, :] = gn[:, sl]

    off = HALO - CF_KERNEL // 2

    def conv_chunk(c, carry):
        w = w_ref[c]
        acc = jnp.zeros((tm, LANES), F32)
        for k in range(CF_KERNEL):
            acc = acc + win_ref[c, pl.ds(off + k, tm), :] * w[k:k + 1, :]
        u_ref[c] = acc + wb_ref[c][0:1]
        return carry

    lax.fori_loop(0, nc, conv_chunk, 0)

    d = nc * LANES
    s1 = u_ref[0]
    for c in range(1, nc):
        s1 = s1 + u_ref[c]
    mu = jnp.sum(s1, axis=-1, keepdims=True) / d
    s2 = jnp.square(u_ref[0] - mu)
    for c in range(1, nc):
        s2 = s2 + jnp.square(u_ref[c] - mu)
    rs = lax.rsqrt(jnp.sum(s2, axis=-1, keepdims=True) / d + LN_EPS)
    chunks = []
    for c in range(nc):
        sl = slice(c * LANES, (c + 1) * LANES)
        y = ((u_ref[c] - mu) * rs) * lg_ref[:, sl] + lb_ref[:, sl]
        zc = z_ref[0, :, sl].astype(F32)
        chunks.append((_silu(y) * _silu(zc)).astype(BF16))
    _residual_out(jnp.concatenate(chunks, axis=1), wo_ref, x_ref, tg_ref, tn_ref, g_ref, xo_ref, ho_ref)


def _halo_specs(tm, d, col, r):
    nblk = tm // HALO
    last_blk = r // HALO - 1
    prev = pl.BlockSpec((1, HALO, d), lambda b, t: (b, jnp.maximum(t * nblk - 1, 0), col))
    cur = pl.BlockSpec((1, tm, d), lambda b, t: (b, t, col))
    nxt = pl.BlockSpec((1, HALO, d), lambda b, t: (b, jnp.minimum((t + 1) * nblk, last_blk), col))
    return [prev, cur, nxt]


def _out_specs_and_args(w_out, x, tab, tab_next, norm_g_next, nlat):
    nb, r, d = x.shape
    row = lambda b, t: (b, t, 0)
    tabspec = pl.BlockSpec((1, 1, 8, d), lambda b, t: (b, t // nlat, 0, 0))
    in_specs = [pl.BlockSpec(w_out.shape, lambda b, t: (0, 0)), pl.BlockSpec((1, ROW_TILE, d), row),
                tabspec, tabspec, pl.BlockSpec((1, d), lambda b, t: (0, 0))]
    args = [w_out, x, tab, tab_next, norm_g_next.reshape(1, d)]
    out_specs = [pl.BlockSpec((1, ROW_TILE, d), row), pl.BlockSpec((1, ROW_TILE, d), row)]
    out_shape = [jax.ShapeDtypeStruct((nb, r, d), F32), jax.ShapeDtypeStruct((nb, r, d), BF16)]
    return in_specs, args, out_specs, out_shape


def _conformer_layer(abz, p, w_out, x, tab, tab_next, norm_g_next, s, l):
    nb, r, d3 = abz.shape
    d = d3 // 3
    tm = ROW_TILE
    assert l == tm and s % tm == 0 and d % LANES == 0
    nc = d // LANES
    nlat = s // tm
    specs = _halo_specs(tm, d, 0, r) + _halo_specs(tm, d, 1, r)
    z_spec = pl.BlockSpec((1, tm, d), lambda b, t: (b, t, 2))
    w = jnp.pad(p["dw_w"].astype(F32), ((0, 1), (0, 0))).reshape(CF_KERNEL + 1, nc, LANES).transpose(1, 0, 2)
    wb = jnp.broadcast_to(p["dw_b"].astype(F32).reshape(nc, 1, LANES), (nc, 8, LANES))
    full = lambda shape: pl.BlockSpec(shape, lambda b, t: (0,) * len(shape))
    o_in, o_args, out_specs, out_shape = _out_specs_and_args(w_out, x, tab, tab_next, norm_g_next, nlat)
    return pl.pallas_call(
        functools.partial(_conf_out_kernel, nlat=nlat),
        grid=(nb, r // tm),
        in_specs=specs + [z_spec, full((nc, CF_KERNEL + 1, LANES)), full((nc, 8, LANES)),
                          full((1, d)), full((1, d))] + o_in,
        out_specs=out_specs,
        out_shape=out_shape,
        scratch_shapes=[pltpu.VMEM((nc, tm + 2 * HALO, LANES), F32), pltpu.VMEM((nc, tm, LANES), F32)],
        compiler_params=_cparams("parallel", "parallel"),
        name="conformer_layer",
    )(abz, abz, abz, abz, abz, abz, abz, w, wb, p["ln_g"].astype(F32).reshape(1, d),
      p["ln_b"].astype(F32).reshape(1, d), *o_args)


def _sc_out_kernel(bg_ref, pc_ref, cc_ref, nc_ref, pv_ref, cv_ref, nv_ref, z_ref, w_ref,
                   wo_ref, x_ref, tg_ref, tn_ref, g_ref, xo_ref, ho_ref, win_ref, *, nlat):
    tm = cc_ref.shape[1]
    pvalid, nvalid = _halo_valid(nlat)

    def prod(c_ref, v_ref):
        return c_ref[0].astype(F32) * v_ref[0].astype(F32)

    win_ref[0:HALO, :] = prod(pc_ref, pv_ref) * pvalid
    win_ref[HALO:HALO + tm, :] = prod(cc_ref, cv_ref)
    win_ref[HALO + tm:2 * HALO + tm, :] = prod(nc_ref, nv_ref) * nvalid
    off = HALO - SC_KERNEL // 2
    w = w_ref[...]
    y = win_ref[pl.ds(off, tm), :] * w[0:1]
    for k in range(1, SC_KERNEL):
        y = y + win_ref[pl.ds(off + k, tm), :] * w[k:k + 1]
    u = ((bg_ref[0].astype(F32) * y) * _silu(z_ref[0].astype(F32))).astype(BF16)
    _residual_out(u, wo_ref, x_ref, tg_ref, tn_ref, g_ref, xo_ref, ho_ref)


def _shortconv_layer(bcvz, p, w_out, x, tab, tab_next, norm_g_next, s, l):
    nb, r, d4 = bcvz.shape
    d = d4 // 4
    tm = ROW_TILE
    assert l == tm and s % tm == 0
    nlat = s // tm
    specs = [pl.BlockSpec((1, tm, d), lambda b, t: (b, t, 0))]
    specs += _halo_specs(tm, d, 1, r) + _halo_specs(tm, d, 2, r)
    specs.append(pl.BlockSpec((1, tm, d), lambda b, t: (b, t, 3)))
    specs.append(pl.BlockSpec((8, d), lambda b, t: (0, 0)))
    w = jnp.pad(p["conv_w"].astype(F32), ((0, 8 - SC_KERNEL), (0, 0)))
    o_in, o_args, out_specs, out_shape = _out_specs_and_args(w_out, x, tab, tab_next, norm_g_next, nlat)
    return pl.pallas_call(
        functools.partial(_sc_out_kernel, nlat=nlat),
        grid=(nb, r // tm),
        in_specs=specs + o_in,
        out_specs=out_specs,
        out_shape=out_shape,
        scratch_shapes=[pltpu.VMEM((tm + 2 * HALO, d), F32)],
        compiler_params=_cparams("parallel", "parallel"),
        name="shortconv_layer",
    )(bcvz, bcvz, bcvz, bcvz, bcvz, bcvz, bcvz, bcvz, w, *o_args)


def _forward(x, c, ctx, c_ctx, layers, final_norm_g):
    nb, s, d = x.shape
    l = ctx.shape[1]
    r = s + l
    assert l == ROW_TILE and s % ROW_TILE == 0 and nb < 8
    nlat = s // ROW_TILE
    depth = len(layers)
    kinds = ("attn", "conformer", "shortconv")

    cond8 = jnp.zeros((8, d), F32).at[:nb].set(c).at[nb].set(c_ctx)
    tabs = [_mod_table(_adaln(cond8, p["ada_w"], p["ada_b"]), nb, d) for p in layers]
    rope = _rope_tables(s, l)

    xs, h = _prologue(x, ctx, tabs[0], layers[0]["norm_g"])
    for i, p in enumerate(layers):
        kind = kinds[i % len(kinds)]
        last = i == depth - 1
        w_in = p["w_in"]
        w_out = p["w_out"].astype(BF16)
        h2 = h.reshape(nb * r, d)
        if kind == "attn":
            qkvz = _in_proj(h2, w_in, r, rope=rope, rope_cols=w_in.shape[1] // 2).reshape(nb, r, -1)
            u = _diff_attention(qkvz, p, i, s, l, keep_ctx=not last)
            if last:
                return _out_proj_final(u, w_out, xs, tabs[i], final_norm_g, s)
            xs, h = _out_proj(u, w_out, xs, tabs[i], tabs[i + 1], layers[i + 1]["norm_g"], nlat)
        else:
            assert not last, "the stack ends with an attention layer"
            proj = _in_proj(h2, w_in, r).reshape(nb, r, -1)
            layer = _conformer_layer if kind == "conformer" else _shortconv_layer
            xs, h = layer(proj, p, w_out, xs, tabs[i], tabs[i + 1], layers[i + 1]["norm_g"], s, l)


def kernel(x, c, ctx, c_ctx, l0_norm_g, l0_ada_w, l0_ada_b, l0_w_in, l0_lam_q1, l0_lam_k1, l0_lam_q2, l0_lam_k2, l0_head_g, l0_w_out, l1_norm_g, l1_ada_w, l1_ada_b, l1_w_in, l1_dw_w, l1_dw_b, l1_ln_g, l1_ln_b, l1_w_out, l2_norm_g, l2_ada_w, l2_ada_b, l2_w_in, l2_conv_w, l2_w_out, l3_norm_g, l3_ada_w, l3_ada_b, l3_w_in, l3_lam_q1, l3_lam_k1, l3_lam_q2, l3_lam_k2, l3_head_g, l3_w_out, final_norm_g):
    layers = [
        dict(norm_g=l0_norm_g, ada_w=l0_ada_w, ada_b=l0_ada_b, w_in=l0_w_in, lam_q1=l0_lam_q1, lam_k1=l0_lam_k1,
             lam_q2=l0_lam_q2, lam_k2=l0_lam_k2, head_g=l0_head_g, w_out=l0_w_out),
        dict(norm_g=l1_norm_g, ada_w=l1_ada_w, ada_b=l1_ada_b, w_in=l1_w_in, dw_w=l1_dw_w, dw_b=l1_dw_b,
             ln_g=l1_ln_g, ln_b=l1_ln_b, w_out=l1_w_out),
        dict(norm_g=l2_norm_g, ada_w=l2_ada_w, ada_b=l2_ada_b, w_in=l2_w_in, conv_w=l2_conv_w, w_out=l2_w_out),
        dict(norm_g=l3_norm_g, ada_w=l3_ada_w, ada_b=l3_ada_b, w_in=l3_w_in, lam_q1=l3_lam_q1, lam_k1=l3_lam_k1,
             lam_q2=l3_lam_q2, lam_k2=l3_lam_k2, head_g=l3_head_g, w_out=l3_w_out),
    ]
    return _forward(x, c, ctx, c_ctx, layers, final_norm_g)
```

```python
import functools
import math

import jax
import jax.numpy as jnp
from jax import lax
from jax.experimental import pallas as pl
from jax.experimental.pallas import tpu as pltpu

F32 = jnp.float32
BF16 = jnp.bfloat16

LANES = 128
HALO = 16
ROW_TILE = 256
DA_HEAD_DIM = 64
HEAD_W = 2 * DA_HEAD_DIM
ROPE_BASE = 10000.0
ROPE_FREQS = DA_HEAD_DIM // 4
GRID_W = 64
CF_KERNEL = 31
SC_KERNEL = 3
NORM_EPS = 1e-6
LN_EPS = 1e-5
Q_TILE = 2048
Q_SUB = 512
FINAL_TILE = 512
KV_CHUNK = 256
Q_SCALE = math.log2(math.e) / math.sqrt(DA_HEAD_DIM)
VMEM_LIMIT = 52 * 1024 * 1024


def _cparams(*sem):
    return pltpu.CompilerParams(dimension_semantics=sem, vmem_limit_bytes=VMEM_LIMIT)


def _silu(x):
    return x * jax.nn.sigmoid(x)


def _ada_kernel(c_ref, w_ref, b_ref, o_ref):
    a = _silu(c_ref[...]).astype(BF16)
    o_ref[...] = jnp.dot(a, w_ref[...].astype(BF16), preferred_element_type=F32) + b_ref[...]


def _adaln(cond8, w, b):
    d, n = w.shape
    tn = min(1024, n)
    return pl.pallas_call(
        _ada_kernel,
        grid=(n // tn,),
        in_specs=[pl.BlockSpec((8, d), lambda j: (0, 0)),
                  pl.BlockSpec((d, tn), lambda j: (0, j)),
                  pl.BlockSpec((1, tn), lambda j: (0, j))],
        out_specs=pl.BlockSpec((8, tn), lambda j: (0, j)),
        out_shape=jax.ShapeDtypeStruct((8, n), F32),
        compiler_params=_cparams("parallel"),
        name="adaln",
    )(cond8, w, b.reshape(1, n))


def _mod_table(m, nb, d):
    sh, sc, g = m[:, :d], m[:, d:2 * d], m[:, 2 * d:]
    rows = jnp.stack([1.0 + sc, sh, g], axis=1)
    lat = rows[:nb]
    ctx = jnp.broadcast_to(rows[nb][None], (nb, 3, d))
    tab = jnp.stack([lat, ctx], axis=1)
    return jnp.pad(tab, ((0, 0), (0, 0), (0, 5), (0, 0)))


def _norm_mod(x, g, scale1p, shift):
    ms = jnp.mean(x * x, axis=-1, keepdims=True)
    y = x * lax.rsqrt(ms + NORM_EPS)
    return (y * g) * scale1p + shift


def _stream_tile(x_ref, c_ref, nlat):
    return jnp.where(pl.program_id(1) < nlat, x_ref[0], c_ref[0])


def _split_stream_specs(d, nlat):
    return [pl.BlockSpec((1, ROW_TILE, d), lambda b, t: (b, jnp.minimum(t, nlat - 1), 0)),
            pl.BlockSpec((1, ROW_TILE, d), lambda b, t: (b, 0, 0))]


def _pro_kernel(x_ref, c_ref, tab_ref, g_ref, h_ref, *, nlat):
    tab = tab_ref[0, 0]
    h_ref[0] = _norm_mod(_stream_tile(x_ref, c_ref, nlat), g_ref[...], tab[0:1], tab[1:2]).astype(BF16)


def _prologue(x, ctx, tab, norm_g):
    nb, s, d = x.shape
    l = ctx.shape[1]
    r = s + l
    nlat = s // ROW_TILE
    row = lambda b, t: (b, t, 0)
    return pl.pallas_call(
        functools.partial(_pro_kernel, nlat=nlat),
        grid=(nb, r // ROW_TILE),
        in_specs=_split_stream_specs(d, nlat) + [
            pl.BlockSpec((1, 1, 8, d), lambda b, t: (b, t // nlat, 0, 0)),
            pl.BlockSpec((1, d), lambda b, t: (0, 0))],
        out_specs=pl.BlockSpec((1, ROW_TILE, d), row),
        out_shape=jax.ShapeDtypeStruct((nb, r, d), BF16),
        compiler_params=_cparams("parallel", "arbitrary"),
        name="prologue",
    )(x, ctx, tab, norm_g.reshape(1, d))


def _in_kernel(h_ref, w_ref, o_ref, wb_ref):
    @pl.when(pl.program_id(1) == 0)
    def _():
        wb_ref[...] = w_ref[...].astype(BF16)

    o_ref[...] = jnp.dot(h_ref[...], wb_ref[...], preferred_element_type=F32).astype(BF16)


def _in_rope_kernel(h_ref, w_ref, cos_ref, sa_ref, sb_ref, o_ref, wb_ref, *, n_rope, tn, sub):
    j = pl.program_id(0)
    tm = h_ref.shape[0]

    @pl.when(pl.program_id(1) == 0)
    def _():
        wb_ref[...] = w_ref[...].astype(BF16)

    @pl.when(j < n_rope)
    def _():
        for r0 in range(0, tm, sub):
            rows = slice(r0, r0 + sub)
            acc = jnp.dot(h_ref[rows, :], wb_ref[...], preferred_element_type=F32)
            cos, sa, sb = cos_ref[rows, :], sa_ref[rows, :], sb_ref[rows, :]
            for c in range(tn // LANES):
                xc = acc[:, c * LANES:(c + 1) * LANES]
                rot = (xc * cos + pltpu.roll(xc, LANES - ROPE_FREQS, 1) * sa
                       + pltpu.roll(xc, ROPE_FREQS, 1) * sb)
                o_ref[rows, c * LANES:(c + 1) * LANES] = rot.astype(BF16)

    @pl.when(j >= n_rope)
    def _():
        o_ref[...] = jnp.dot(h_ref[...], wb_ref[...], preferred_element_type=F32).astype(BF16)


def _in_tiles(t, r, d, n):
    tm = r // 4
    tn = 1024 if d >= 1024 else d
    assert tm % HALO == 0 and t % tm == 0 and n % tn == 0
    return tm, tn


def _in_proj(h, w, r, rope=None, rope_cols=0):
    t, d = h.shape
    n = w.shape[1]
    tm, tn = _in_tiles(t, r, d, n)
    grid = (n // tn, t // tm)
    h_spec = pl.BlockSpec((tm, d), lambda j, i: (i, 0))
    w_spec = pl.BlockSpec((d, tn), lambda j, i: (0, j))
    o_spec = pl.BlockSpec((tm, tn), lambda j, i: (i, j))
    out_shape = jax.ShapeDtypeStruct((t, n), BF16)
    scratch = [pltpu.VMEM((d, tn), BF16)]
    if rope is None:
        return pl.pallas_call(_in_kernel, grid=grid, in_specs=[h_spec, w_spec], out_specs=o_spec,
                              out_shape=out_shape, scratch_shapes=scratch,
                              compiler_params=_cparams("parallel", "arbitrary"), name="in_proj")(h, w)
    assert rope_cols % (2 * tn) == 0 and tn % LANES == 0
    per = r // tm
    n_q = rope_cols // (2 * tn)
    t_spec = pl.BlockSpec((tm, LANES), lambda j, i: (i % per + jnp.where(j >= n_q, per, 0), 0))
    sub = tm // 4 if (tm // 4) % HALO == 0 else tm
    kern = functools.partial(_in_rope_kernel, n_rope=rope_cols // tn, tn=tn, sub=sub)
    return pl.pallas_call(kern, grid=grid, in_specs=[h_spec, w_spec, t_spec, t_spec, t_spec],
                          out_specs=o_spec, out_shape=out_shape, scratch_shapes=scratch,
                          compiler_params=_cparams("parallel", "arbitrary"),
                          name="in_proj_rope")(h, w, *rope)


def _rope_tables(s, l):
    t = jnp.arange(s)
    row = (t // GRID_W).astype(F32)
    col = (t % GRID_W).astype(F32)
    inv = ROPE_BASE ** (-jnp.arange(ROPE_FREQS, dtype=F32) / ROPE_FREQS)
    lane = jnp.arange(LANES)
    dd = lane % DA_HEAD_DIM
    axis, half, f = dd // (2 * ROPE_FREQS), (dd % (2 * ROPE_FREQS)) // ROPE_FREQS, dd % ROPE_FREQS
    pos = jnp.where(axis[None, :] == 0, row[:, None], col[:, None])
    ang = pos * inv[f][None, :]
    cos, sin = jnp.cos(ang), jnp.sin(ang)
    sa = jnp.where(half[None, :] == 0, -sin, 0.0)
    sb = jnp.where(half[None, :] == 1, sin, 0.0)
    cos = jnp.concatenate([cos, jnp.ones((l, LANES), F32)], axis=0)
    sa = jnp.concatenate([sa, jnp.zeros((l, LANES), F32)], axis=0)
    sb = jnp.concatenate([sb, jnp.zeros((l, LANES), F32)], axis=0)
    return tuple(jnp.concatenate([t * Q_SCALE, t], axis=0) for t in (cos, sa, sb))


V_ROWS = HEAD_W + HALO


SAFE_DENOM = 2.0 ** -100
BOUND_SLACK = 1.01


def _split_comps(q):
    lane = lax.broadcasted_iota(jnp.int32, q.shape, 1)
    return jnp.concatenate([jnp.where((lane // DA_HEAD_DIM) == c, q, jnp.zeros_like(q)) for c in range(2)], axis=0)


def _nt_dot(a, b):
    return lax.dot_general(a, b, (((1,), (1,)), ((), ())), preferred_element_type=F32)


def _exact_chunks(k_ref, vt_ref, q01, chunks, ck):
    def scores(j):
        return _nt_dot(k_ref[0, j * ck:(j + 1) * ck, :], q01)

    s_next = scores(chunks[0])
    m = acc = None
    for i, j in enumerate(chunks):
        s = s_next
        if i + 1 < len(chunks):
            s_next = scores(chunks[i + 1])
        mc = jnp.max(s, axis=0, keepdims=True)
        m_new = mc if m is None else jnp.maximum(m, mc)
        p = jnp.exp2(s - m_new).astype(BF16)
        pv = jnp.dot(vt_ref[:, j * ck:(j + 1) * ck], p, preferred_element_type=F32)
        acc = pv if m is None else acc * jnp.exp2(m - m_new) + pv
        m = m_new
    return acc


def _bound_chunks(k_ref, vt_ref, q01, mb, chunks, ck):
    def scores(j):
        return _nt_dot(k_ref[0, j * ck:(j + 1) * ck, :], q01)

    acc = None
    s_next = scores(chunks[0])
    for i, j in enumerate(chunks):
        s = s_next
        if i + 1 < len(chunks):
            s_next = scores(chunks[i + 1])
        p = jnp.exp2(s - mb).astype(BF16)
        pv = jnp.dot(vt_ref[:, j * ck:(j + 1) * ck], p, preferred_element_type=F32)
        acc = pv if acc is None else acc + pv
    return acc


def _attn_finish(acc, lam, hg, z, lam_init):
    t = acc.shape[1] // 2
    o = acc[0:HEAD_W] * (1.0 / acc[HEAD_W:HEAD_W + 1])
    d = o[:, 0:t] - lam * o[:, t:2 * t]
    ms = jnp.mean(d * d, axis=0, keepdims=True)
    dn = (d * lax.rsqrt(ms + NORM_EPS)).T
    on = (dn * hg) * (1.0 - lam_init)
    return (on * _silu(z.astype(F32))).astype(BF16)


def _attn_body(lam_ref, hg_ref, q_ref, qc_ref, k_ref, v_ref, z_ref, zc_ref, o_ref, vt_ref, kn_ref, *,
               lam_init, ck, sub, keep_ctx):
    qi = pl.program_id(2)
    nkv = k_ref.shape[1]
    nchunk = nkv // ck

    @pl.when(qi == 0)
    def _():
        vt_ref[0:HEAD_W, :] = v_ref[0].astype(F32).T.astype(BF16)
        vt_ref[HEAD_W:V_ROWS, :] = jnp.ones((HALO, nkv), BF16)
        kf = k_ref[0].astype(F32)
        r_i = lax.broadcasted_iota(jnp.int32, (HEAD_W, HEAD_W), 0) // DA_HEAD_DIM
        c_i = lax.broadcasted_iota(jnp.int32, (HEAD_W, HEAD_W), 1) // DA_HEAD_DIM
        sel = jnp.where(r_i == c_i, 1.0, 0.0).astype(BF16)
        kn2 = jnp.dot((kf * kf).astype(BF16), sel, preferred_element_type=F32)
        kn_ref[...] = jnp.broadcast_to(jnp.sqrt(jnp.max(kn2, axis=0, keepdims=True)), kn_ref.shape)

    lp = lam_ref[...]
    lam = (jnp.exp(jnp.sum(lp[0:1] * lp[1:2], axis=-1, keepdims=True))
           - jnp.exp(jnp.sum(lp[2:3] * lp[3:4], axis=-1, keepdims=True)) + lam_init)
    hg = hg_ref[...]

    if keep_ctx:
        @pl.when(qi == 0)
        def _():
            acc = _exact_chunks(k_ref, vt_ref, _split_comps(qc_ref[0]), [nchunk - 1], ck)
            o_ref[0, 0:qc_ref.shape[1], :] = _attn_finish(acc, lam, hg, zc_ref[0], lam_init)

    @pl.when(qi >= (1 if keep_ctx else 0))
    def _():
        nsub = q_ref.shape[1] // sub
        kn = kn_ref[0:1, :]
        kmax = jnp.concatenate([jnp.broadcast_to(kn[:, 0:1], (1, sub)),
                                jnp.broadcast_to(kn[:, DA_HEAD_DIM:DA_HEAD_DIM + 1], (1, sub))], axis=1)
        denoms = []
        for t in range(nsub):
            rows = slice(t * sub, (t + 1) * sub)
            q01 = _split_comps(q_ref[0, rows, :])
            qf = q01.astype(F32)
            qn2 = _nt_dot(jnp.ones((HALO, HEAD_W), BF16), (qf * qf).astype(BF16))[0:1]
            mb = jnp.sqrt(qn2) * kmax * BOUND_SLACK
            acc = _bound_chunks(k_ref, vt_ref, q01, mb, range(nchunk), ck)
            o_ref[0, rows, :] = _attn_finish(acc, lam, hg, z_ref[0, rows, :], lam_init)
            denoms.append(jnp.min(acc[HEAD_W:HEAD_W + 1]))
        safe = functools.reduce(jnp.minimum, denoms) >= SAFE_DENOM

        @pl.when(jnp.logical_not(safe))
        def _():
            def redo(t, carry):
                rows = pl.ds(pl.multiple_of(t * sub, sub), sub)
                acc2 = _exact_chunks(k_ref, vt_ref, _split_comps(q_ref[0, rows, :]), range(nchunk), ck)
                o_ref[0, rows, :] = _attn_finish(acc2, lam, hg, z_ref[0, rows, :], lam_init)
                return carry

            lax.fori_loop(0, nsub, redo, 0)


def _diff_attention(qkvz, p, layer_idx, s, l, keep_ctx):
    nb, r, n4 = qkvz.shape
    nh = n4 // (4 * HEAD_W)
    lam_init = 0.8 - 0.6 * math.exp(-0.3 * layer_idx)
    lamp = jnp.stack([p["lam_q1"], p["lam_k1"], p["lam_q2"], p["lam_k2"]]).astype(F32)
    lamp = jnp.pad(lamp, ((0, 4), (0, LANES - DA_HEAD_DIM)))
    hg = p["head_g"].astype(F32).reshape(1, HEAD_W)
    tq = min(Q_TILE, s)
    ck = KV_CHUNK
    sub = min(Q_SUB, tq)
    assert s % tq == 0 and tq % sub == 0 and l == ck and r % ck == 0
    nlat_q = s // tq
    n_pre = 1 if keep_ctx else 0
    ctx_blk = s // l
    kern = functools.partial(_attn_body, lam_init=lam_init, ck=ck, sub=sub, keep_ctx=keep_ctx)
    const = lambda b, h, i: (0, 0)
    lat = lambda col: pl.BlockSpec((1, tq, HEAD_W),
                                   lambda b, h, i: (b, jnp.maximum(i - n_pre, 0), col * nh + h))
    ctx = lambda col: pl.BlockSpec((1, l, HEAD_W), lambda b, h, i: (b, ctx_blk, col * nh + h))
    kv = lambda col: pl.BlockSpec((1, r, HEAD_W), lambda b, h, i: (b, 0, col * nh + h))
    out_map = lambda b, h, i: (b, jnp.where(i < n_pre, nlat_q, i - n_pre), h)
    return pl.pallas_call(
        kern,
        grid=(nb, nh, nlat_q + n_pre),
        in_specs=[pl.BlockSpec((8, LANES), const), pl.BlockSpec((1, LANES), const),
                  lat(0), ctx(0), kv(1), kv(2), lat(3), ctx(3)],
        out_specs=pl.BlockSpec((1, tq, HEAD_W), out_map),
        out_shape=jax.ShapeDtypeStruct((nb, r if keep_ctx else s, nh * HEAD_W), BF16),
        scratch_shapes=[pltpu.VMEM((V_ROWS, r), BF16), pltpu.VMEM((8, LANES), F32)],
        compiler_params=_cparams("parallel", "parallel", "arbitrary"),
        name="diff_attn",
    )(lamp, hg, qkvz, qkvz, qkvz, qkvz, qkvz, qkvz)


MXU_K = 256


def _residual_out(u, w_ref, x, tg_ref, tn_ref, g_ref, xo_ref, ho_ref, k_major=False):
    if k_major:
        y = None
        for k0 in range(0, u.shape[1], MXU_K):
            part = jnp.dot(u[:, k0:k0 + MXU_K], w_ref[k0:k0 + MXU_K, :], preferred_element_type=F32)
            y = part if y is None else y + part
    else:
        y = jnp.dot(u, w_ref[...], preferred_element_type=F32)
    xn = x + tg_ref[0, 0][2:3] * y
    xo_ref[0] = xn
    tab = tn_ref[0, 0]
    ho_ref[0] = _norm_mod(xn, g_ref[...], tab[0:1], tab[1:2]).astype(BF16)


def _out_kernel(u_ref, w_ref, x_ref, tg_ref, tn_ref, g_ref, xo_ref, ho_ref):
    _residual_out(u_ref[0], w_ref, x_ref[0], tg_ref, tn_ref, g_ref, xo_ref, ho_ref)


def _out_split_kernel(u_ref, w_ref, x_ref, c_ref, tg_ref, tn_ref, g_ref, xo_ref, ho_ref, *, nlat):
    _residual_out(u_ref[0], w_ref, _stream_tile(x_ref, c_ref, nlat), tg_ref, tn_ref, g_ref, xo_ref, ho_ref)


def _out_final_kernel(u_ref, w_ref, x_ref, tg_ref, g_ref, o_ref):
    y = jnp.dot(u_ref[0], w_ref[...], preferred_element_type=F32)
    xn = x_ref[0] + tg_ref[0, 0][2:3] * y
    ms = jnp.mean(xn * xn, axis=-1, keepdims=True)
    o_ref[0] = (xn * lax.rsqrt(ms + NORM_EPS)) * g_ref[...]


def _out_proj(u, w, x, tab, tab_next, norm_g_next, nlat):
    nb, r, dw = u.shape
    d = w.shape[1]
    row = lambda b, t: (b, t, 0)
    tabspec = pl.BlockSpec((1, 1, 8, d), lambda b, t: (b, t // nlat, 0, 0))
    if isinstance(x, tuple):
        kern, x_args, x_specs = functools.partial(_out_split_kernel, nlat=nlat), list(x), _split_stream_specs(d, nlat)
    else:
        kern, x_args, x_specs = _out_kernel, [x], [pl.BlockSpec((1, ROW_TILE, d), row)]
    return pl.pallas_call(
        kern,
        grid=(nb, r // ROW_TILE),
        in_specs=[pl.BlockSpec((1, ROW_TILE, dw), row),
                  pl.BlockSpec((dw, d), lambda b, t: (0, 0))] + x_specs + [
                  tabspec, tabspec,
                  pl.BlockSpec((1, d), lambda b, t: (0, 0))],
        out_specs=[pl.BlockSpec((1, ROW_TILE, d), row), pl.BlockSpec((1, ROW_TILE, d), row)],
        out_shape=[jax.ShapeDtypeStruct((nb, r, d), F32), jax.ShapeDtypeStruct((nb, r, d), BF16)],
        compiler_params=_cparams("parallel", "parallel"),
        name="out_proj",
    )(u, w, *x_args, tab, tab_next, norm_g_next.reshape(1, d))


def _out_proj_final(u, w, x, tab, final_g, s):
    nb, r, d = x.shape
    dw = u.shape[2]
    row = lambda b, t: (b, t, 0)
    tm = FINAL_TILE if s % FINAL_TILE == 0 else ROW_TILE
    return pl.pallas_call(
        _out_final_kernel,
        grid=(nb, s // tm),
        in_specs=[pl.BlockSpec((1, tm, dw), row),
                  pl.BlockSpec((dw, d), lambda b, t: (0, 0)),
                  pl.BlockSpec((1, tm, d), row),
                  pl.BlockSpec((1, 1, 8, d), lambda b, t: (b, 0, 0, 0)),
                  pl.BlockSpec((1, d), lambda b, t: (0, 0))],
        out_specs=pl.BlockSpec((1, tm, d), row),
        out_shape=jax.ShapeDtypeStruct((nb, s, d), F32),
        compiler_params=_cparams("parallel", "parallel"),
        name="out_proj_final",
    )(u, w, x, tab, final_g.reshape(1, d))


def _halo_valid(nlat):
    t = pl.program_id(1)
    prev_ok = jnp.logical_and(t != 0, t != nlat)
    next_ok = jnp.logical_and(t != nlat - 1, t != nlat)
    return prev_ok.astype(F32), next_ok.astype(F32)


def _conf_out_kernel(pa_ref, ca_ref, na_ref, pb_ref, cb_ref, nb_ref, z_ref, w_ref, wb_ref, lg_ref, lb_ref,
                     wo_ref, x_ref, tg_ref, tn_ref, g_ref, xo_ref, ho_ref, win_ref, u_ref, *, nlat):
    tm = ca_ref.shape[1]
    nc = win_ref.shape[0]
    pv, nv = _halo_valid(nlat)

    def glu(a_ref, b_ref):
        return a_ref[0].astype(F32) * jax.nn.sigmoid(b_ref[0].astype(F32))

    gp, gc, gn = glu(pa_ref, pb_ref) * pv, glu(ca_ref, cb_ref), glu(na_ref, nb_ref) * nv
    for c in range(nc):
        sl = slice(c * LANES, (c + 1) * LANES)
        win_ref[c, 0:HALO, :] = gp[:, sl]
        win_ref[c, HALO:HALO + tm, :] = gc[:, sl]
        win_ref[c, HALO + tm:2 * HALO + tm, :] = gn[:, sl]

    off = HALO - CF_KERNEL // 2

    def conv_chunk(c, carry):
        w = w_ref[c]
        acc = jnp.zeros((tm, LANES), F32)
        for k in range(CF_KERNEL):
            acc = acc + win_ref[c, pl.ds(off + k, tm), :] * w[k:k + 1, :]
        u_ref[c] = acc + wb_ref[c][0:1]
        return carry

    lax.fori_loop(0, nc, conv_chunk, 0)

    d = nc * LANES
    s1 = u_ref[0]
    for c in range(1, nc):
        s1 = s1 + u_ref[c]
    mu = jnp.sum(s1, axis=-1, keepdims=True) / d
    s2 = jnp.square(u_ref[0] - mu)
    for c in range(1, nc):
        s2 = s2 + jnp.square(u_ref[c] - mu)
    rs = lax.rsqrt(jnp.sum(s2, axis=-1, keepdims=True) / d + LN_EPS)
    chunks = []
    for c in range(nc):
        sl = slice(c * LANES, (c + 1) * LANES)
        y = ((u_ref[c] - mu) * rs) * lg_ref[:, sl] + lb_ref[:, sl]
        zc = z_ref[0, :, sl].astype(F32)
        chunks.append((_silu(y) * _silu(zc)).astype(BF16))
    _residual_out(jnp.concatenate(chunks, axis=1), wo_ref, x_ref[0], tg_ref, tn_ref, g_ref, xo_ref, ho_ref,
                  k_major=True)


def _halo_specs(tm, d, col, r):
    nblk = tm // HALO
    last_blk = r // HALO - 1
    prev = pl.BlockSpec((1, HALO, d), lambda b, t: (b, jnp.maximum(t * nblk - 1, 0), col))
    cur = pl.BlockSpec((1, tm, d), lambda b, t: (b, t, col))
    nxt = pl.BlockSpec((1, HALO, d), lambda b, t: (b, jnp.minimum((t + 1) * nblk, last_blk), col))
    return [prev, cur, nxt]


def _out_specs_and_args(w_out, x, tab, tab_next, norm_g_next, nlat):
    nb, r, d = x.shape
    row = lambda b, t: (b, t, 0)
    tabspec = pl.BlockSpec((1, 1, 8, d), lambda b, t: (b, t // nlat, 0, 0))
    in_specs = [pl.BlockSpec(w_out.shape, lambda b, t: (0, 0)), pl.BlockSpec((1, ROW_TILE, d), row),
                tabspec, tabspec, pl.BlockSpec((1, d), lambda b, t: (0, 0))]
    args = [w_out, x, tab, tab_next, norm_g_next.reshape(1, d)]
    out_specs = [pl.BlockSpec((1, ROW_TILE, d), row), pl.BlockSpec((1, ROW_TILE, d), row)]
    out_shape = [jax.ShapeDtypeStruct((nb, r, d), F32), jax.ShapeDtypeStruct((nb, r, d), BF16)]
    return in_specs, args, out_specs, out_shape


def _conformer_layer(abz, p, w_out, x, tab, tab_next, norm_g_next, s, l):
    nb, r, d3 = abz.shape
    d = d3 // 3
    tm = ROW_TILE
    assert l == tm and s % tm == 0 and d % LANES == 0
    nc = d // LANES
    nlat = s // tm
    specs = _halo_specs(tm, d, 0, r) + _halo_specs(tm, d, 1, r)
    z_spec = pl.BlockSpec((1, tm, d), lambda b, t: (b, t, 2))
    w = jnp.pad(p["dw_w"].astype(F32), ((0, 1), (0, 0))).reshape(CF_KERNEL + 1, nc, LANES).transpose(1, 0, 2)
    wb = jnp.broadcast_to(p["dw_b"].astype(F32).reshape(nc, 1, LANES), (nc, 8, LANES))
    full = lambda shape: pl.BlockSpec(shape, lambda b, t: (0,) * len(shape))
    o_in, o_args, out_specs, out_shape = _out_specs_and_args(w_out, x, tab, tab_next, norm_g_next, nlat)
    return pl.pallas_call(
        functools.partial(_conf_out_kernel, nlat=nlat),
        grid=(nb, r // tm),
        in_specs=specs + [z_spec, full((nc, CF_KERNEL + 1, LANES)), full((nc, 8, LANES)),
                          full((1, d)), full((1, d))] + o_in,
        out_specs=out_specs,
        out_shape=out_shape,
        scratch_shapes=[pltpu.VMEM((nc, tm + 2 * HALO, LANES), F32), pltpu.VMEM((nc, tm, LANES), F32)],
        compiler_params=_cparams("parallel", "parallel"),
        name="conformer_layer",
    )(abz, abz, abz, abz, abz, abz, abz, w, wb, p["ln_g"].astype(F32).reshape(1, d),
      p["ln_b"].astype(F32).reshape(1, d), *o_args)


def _sc_out_kernel(bg_ref, pc_ref, cc_ref, nc_ref, pv_ref, cv_ref, nv_ref, z_ref, w_ref,
                   wo_ref, x_ref, tg_ref, tn_ref, g_ref, xo_ref, ho_ref, win_ref, *, nlat):
    tm = cc_ref.shape[1]
    pvalid, nvalid = _halo_valid(nlat)

    def prod(c_ref, v_ref):
        return c_ref[0].astype(F32) * v_ref[0].astype(F32)

    win_ref[0:HALO, :] = prod(pc_ref, pv_ref) * pvalid
    win_ref[HALO:HALO + tm, :] = prod(cc_ref, cv_ref)
    win_ref[HALO + tm:2 * HALO + tm, :] = prod(nc_ref, nv_ref) * nvalid
    off = HALO - SC_KERNEL // 2
    w = w_ref[...]
    y = win_ref[pl.ds(off, tm), :] * w[0:1]
    for k in range(1, SC_KERNEL):
        y = y + win_ref[pl.ds(off + k, tm), :] * w[k:k + 1]
    u = ((bg_ref[0].astype(F32) * y) * _silu(z_ref[0].astype(F32))).astype(BF16)
    _residual_out(u, wo_ref, x_ref[0], tg_ref, tn_ref, g_ref, xo_ref, ho_ref, k_major=True)


def _shortconv_layer(bcvz, p, w_out, x, tab, tab_next, norm_g_next, s, l):
    nb, r, d4 = bcvz.shape
    d = d4 // 4
    tm = ROW_TILE
    assert l == tm and s % tm == 0
    nlat = s // tm
    specs = [pl.BlockSpec((1, tm, d), lambda b, t: (b, t, 0))]
    specs += _halo_specs(tm, d, 1, r) + _halo_specs(tm, d, 2, r)
    specs.append(pl.BlockSpec((1, tm, d), lambda b, t: (b, t, 3)))
    specs.append(pl.BlockSpec((8, d), lambda b, t: (0, 0)))
    w = jnp.pad(p["conv_w"].astype(F32), ((0, 8 - SC_KERNEL), (0, 0)))
    o_in, o_args, out_specs, out_shape = _out_specs_and_args(w_out, x, tab, tab_next, norm_g_next, nlat)
    return pl.pallas_call(
        functools.partial(_sc_out_kernel, nlat=nlat),
        grid=(nb, r // tm),
        in_specs=specs + o_in,
        out_specs=out_specs,
        out_shape=out_shape,
        scratch_shapes=[pltpu.VMEM((tm + 2 * HALO, d), F32)],
        compiler_params=_cparams("parallel", "parallel"),
        name="shortconv_layer",
    )(bcvz, bcvz, bcvz, bcvz, bcvz, bcvz, bcvz, bcvz, w, *o_args)


def _forward(x, c, ctx, c_ctx, layers, final_norm_g):
    nb, s, d = x.shape
    l = ctx.shape[1]
    r = s + l
    assert l == ROW_TILE and s % ROW_TILE == 0 and nb < 8
    nlat = s // ROW_TILE
    depth = len(layers)
    kinds = ("attn", "conformer", "shortconv")

    cond8 = jnp.zeros((8, d), F32).at[:nb].set(c).at[nb].set(c_ctx)
    tabs = [_mod_table(_adaln(cond8, p["ada_w"], p["ada_b"]), nb, d) for p in layers]
    rope = _rope_tables(s, l)

    h = _prologue(x, ctx, tabs[0], layers[0]["norm_g"])
    xs = (x, ctx)
    for i, p in enumerate(layers):
        kind = kinds[i % len(kinds)]
        last = i == depth - 1
        w_in = p["w_in"]
        w_out = p["w_out"].astype(BF16)
        h2 = h.reshape(nb * r, d)
        if kind == "attn":
            qkvz = _in_proj(h2, w_in, r, rope=rope, rope_cols=w_in.shape[1] // 2).reshape(nb, r, -1)
            u = _diff_attention(qkvz, p, i, s, l, keep_ctx=not last)
            if last:
                return _out_proj_final(u, w_out, xs, tabs[i], final_norm_g, s)
            xs, h = _out_proj(u, w_out, xs, tabs[i], tabs[i + 1], layers[i + 1]["norm_g"], nlat)
        else:
            assert not last and i > 0, "the stack starts and ends with an attention layer"
            proj = _in_proj(h2, w_in, r).reshape(nb, r, -1)
            layer = _conformer_layer if kind == "conformer" else _shortconv_layer
            xs, h = layer(proj, p, w_out, xs, tabs[i], tabs[i + 1], layers[i + 1]["norm_g"], s, l)


def kernel(x, c, ctx, c_ctx, l0_norm_g, l0_ada_w, l0_ada_b, l0_w_in, l0_lam_q1, l0_lam_k1, l0_lam_q2, l0_lam_k2, l0_head_g, l0_w_out, l1_norm_g, l1_ada_w, l1_ada_b, l1_w_in, l1_dw_w, l1_dw_b, l1_ln_g, l1_ln_b, l1_w_out, l2_norm_g, l2_ada_w, l2_ada_b, l2_w_in, l2_conv_w, l2_w_out, l3_norm_g, l3_ada_w, l3_ada_b, l3_w_in, l3_lam_q1, l3_lam_k1, l3_lam_q2, l3_lam_k2, l3_head_g, l3_w_out, final_norm_g):
    layers = [
        dict(norm_g=l0_norm_g, ada_w=l0_ada_w, ada_b=l0_ada_b, w_in=l0_w_in, lam_q1=l0_lam_q1, lam_k1=l0_lam_k1,
             lam_q2=l0_lam_q2, lam_k2=l0_lam_k2, head_g=l0_head_g, w_out=l0_w_out),
        dict(norm_g=l1_norm_g, ada_w=l1_ada_w, ada_b=l1_ada_b, w_in=l1_w_in, dw_w=l1_dw_w, dw_b=l1_dw_b,
             ln_g=l1_ln_g, ln_b=l1_ln_b, w_out=l1_w_out),
        dict(norm_g=l2_norm_g, ada_w=l2_ada_w, ada_b=l2_ada_b, w_in=l2_w_in, conv_w=l2_conv_w, w_out=l2_w_out),
        dict(norm_g=l3_norm_g, ada_w=l3_ada_w, ada_b=l3_ada_b, w_in=l3_w_in, lam_q1=l3_lam_q1, lam_k1=l3_lam_k1,
             lam_q2=l3_lam_q2, lam_k2=l3_lam_k2, head_g=l3_head_g, w_out=l3_w_out),
    ]
    return _forward(x, c, ctx, c_ctx, layers, final_norm_g)
```

```python
import functools
import math

import jax
import jax.numpy as jnp
from jax import lax
from jax.experimental import pallas as pl
from jax.experimental.pallas import tpu as pltpu

F32 = jnp.float32
BF16 = jnp.bfloat16

LANES = 128
HALO = 16
ROW_TILE = 256
DA_HEAD_DIM = 64
HEAD_W = 2 * DA_HEAD_DIM
ROPE_BASE = 10000.0
ROPE_FREQS = DA_HEAD_DIM // 4
GRID_W = 64
CF_KERNEL = 31
SC_KERNEL = 3
NORM_EPS = 1e-6
LN_EPS = 1e-5
Q_TILE = 4096
Q_SUB = 512
FINAL_TILE = 512
KV_CHUNK = 256
Q_SCALE = math.log2(math.e) / math.sqrt(DA_HEAD_DIM)
VMEM_LIMIT = 52 * 1024 * 1024


def _cparams(*sem):
    return pltpu.CompilerParams(dimension_semantics=sem, vmem_limit_bytes=VMEM_LIMIT)


def _silu(x):
    return x * jax.nn.sigmoid(x)


def _ada_kernel(c_ref, w_ref, b_ref, o_ref):
    a = _silu(c_ref[...]).astype(BF16)
    o_ref[...] = jnp.dot(a, w_ref[...].astype(BF16), preferred_element_type=F32) + b_ref[...]


def _adaln(cond8, w, b):
    d, n = w.shape
    tn = min(1024, n)
    return pl.pallas_call(
        _ada_kernel,
        grid=(n // tn,),
        in_specs=[pl.BlockSpec((8, d), lambda j: (0, 0)),
                  pl.BlockSpec((d, tn), lambda j: (0, j)),
                  pl.BlockSpec((1, tn), lambda j: (0, j))],
        out_specs=pl.BlockSpec((8, tn), lambda j: (0, j)),
        out_shape=jax.ShapeDtypeStruct((8, n), F32),
        compiler_params=_cparams("parallel"),
        name="adaln",
    )(cond8, w, b.reshape(1, n))


def _mod_table(m, nb, d):
    sh, sc, g = m[:, :d], m[:, d:2 * d], m[:, 2 * d:]
    rows = jnp.stack([1.0 + sc, sh, g], axis=1)
    lat = rows[:nb]
    ctx = jnp.broadcast_to(rows[nb][None], (nb, 3, d))
    tab = jnp.stack([lat, ctx], axis=1)
    return jnp.pad(tab, ((0, 0), (0, 0), (0, 5), (0, 0)))


def _norm_mod(x, g, scale1p, shift):
    ms = jnp.mean(x * x, axis=-1, keepdims=True)
    y = x * lax.rsqrt(ms + NORM_EPS)
    return (y * g) * scale1p + shift


def _stream_tile(x_ref, c_ref, nlat):
    return jnp.where(pl.program_id(1) < nlat, x_ref[0], c_ref[0])


def _split_stream_specs(d, nlat):
    return [pl.BlockSpec((1, ROW_TILE, d), lambda b, t: (b, jnp.minimum(t, nlat - 1), 0)),
            pl.BlockSpec((1, ROW_TILE, d), lambda b, t: (b, 0, 0))]


def _pro_kernel(x_ref, c_ref, tab_ref, g_ref, h_ref, *, nlat):
    tab = tab_ref[0, 0]
    h_ref[0] = _norm_mod(_stream_tile(x_ref, c_ref, nlat), g_ref[...], tab[0:1], tab[1:2]).astype(BF16)


def _prologue(x, ctx, tab, norm_g):
    nb, s, d = x.shape
    l = ctx.shape[1]
    r = s + l
    nlat = s // ROW_TILE
    row = lambda b, t: (b, t, 0)
    return pl.pallas_call(
        functools.partial(_pro_kernel, nlat=nlat),
        grid=(nb, r // ROW_TILE),
        in_specs=_split_stream_specs(d, nlat) + [
            pl.BlockSpec((1, 1, 8, d), lambda b, t: (b, t // nlat, 0, 0)),
            pl.BlockSpec((1, d), lambda b, t: (0, 0))],
        out_specs=pl.BlockSpec((1, ROW_TILE, d), row),
        out_shape=jax.ShapeDtypeStruct((nb, r, d), BF16),
        compiler_params=_cparams("parallel", "arbitrary"),
        name="prologue",
    )(x, ctx, tab, norm_g.reshape(1, d))


def _in_kernel(h_ref, w_ref, o_ref, wb_ref):
    @pl.when(pl.program_id(1) == 0)
    def _():
        wb_ref[...] = w_ref[...].astype(BF16)

    o_ref[...] = jnp.dot(h_ref[...], wb_ref[...], preferred_element_type=F32).astype(BF16)


def _in_rope_kernel(h_ref, w_ref, cos_ref, sa_ref, sb_ref, o_ref, wb_ref, *, n_rope, tn, sub):
    j = pl.program_id(0)
    tm = h_ref.shape[0]

    @pl.when(pl.program_id(1) == 0)
    def _():
        wb_ref[...] = w_ref[...].astype(BF16)

    @pl.when(j < n_rope)
    def _():
        for r0 in range(0, tm, sub):
            rows = slice(r0, r0 + sub)
            acc = jnp.dot(h_ref[rows, :], wb_ref[...], preferred_element_type=F32)
            cos, sa, sb = cos_ref[rows, :], sa_ref[rows, :], sb_ref[rows, :]
            for c in range(tn // LANES):
                xc = acc[:, c * LANES:(c + 1) * LANES]
                rot = (xc * cos + pltpu.roll(xc, LANES - ROPE_FREQS, 1) * sa
                       + pltpu.roll(xc, ROPE_FREQS, 1) * sb)
                o_ref[rows, c * LANES:(c + 1) * LANES] = rot.astype(BF16)

    @pl.when(j >= n_rope)
    def _():
        o_ref[...] = jnp.dot(h_ref[...], wb_ref[...], preferred_element_type=F32).astype(BF16)


def _in_tiles(t, r, d, n):
    tm = r // 4
    tn = 1024 if d >= 1024 else d
    assert tm % HALO == 0 and t % tm == 0 and n % tn == 0
    return tm, tn


def _in_proj(h, w, r, rope=None, rope_cols=0):
    t, d = h.shape
    n = w.shape[1]
    tm, tn = _in_tiles(t, r, d, n)
    grid = (n // tn, t // tm)
    h_spec = pl.BlockSpec((tm, d), lambda j, i: (i, 0))
    w_spec = pl.BlockSpec((d, tn), lambda j, i: (0, j))
    o_spec = pl.BlockSpec((tm, tn), lambda j, i: (i, j))
    out_shape = jax.ShapeDtypeStruct((t, n), BF16)
    scratch = [pltpu.VMEM((d, tn), BF16)]
    if rope is None:
        return pl.pallas_call(_in_kernel, grid=grid, in_specs=[h_spec, w_spec], out_specs=o_spec,
                              out_shape=out_shape, scratch_shapes=scratch,
                              compiler_params=_cparams("parallel", "arbitrary"), name="in_proj")(h, w)
    assert rope_cols % (2 * tn) == 0 and tn % LANES == 0
    per = r // tm
    n_q = rope_cols // (2 * tn)
    t_spec = pl.BlockSpec((tm, LANES), lambda j, i: (i % per + jnp.where(j >= n_q, per, 0), 0))
    sub = tm // 4 if (tm // 4) % HALO == 0 else tm
    kern = functools.partial(_in_rope_kernel, n_rope=rope_cols // tn, tn=tn, sub=sub)
    return pl.pallas_call(kern, grid=grid, in_specs=[h_spec, w_spec, t_spec, t_spec, t_spec],
                          out_specs=o_spec, out_shape=out_shape, scratch_shapes=scratch,
                          compiler_params=_cparams("parallel", "arbitrary"),
                          name="in_proj_rope")(h, w, *rope)


def _rope_tables(s, l):
    t = jnp.arange(s)
    row = (t // GRID_W).astype(F32)
    col = (t % GRID_W).astype(F32)
    inv = ROPE_BASE ** (-jnp.arange(ROPE_FREQS, dtype=F32) / ROPE_FREQS)
    lane = jnp.arange(LANES)
    dd = lane % DA_HEAD_DIM
    axis, half, f = dd // (2 * ROPE_FREQS), (dd % (2 * ROPE_FREQS)) // ROPE_FREQS, dd % ROPE_FREQS
    pos = jnp.where(axis[None, :] == 0, row[:, None], col[:, None])
    ang = pos * inv[f][None, :]
    cos, sin = jnp.cos(ang), jnp.sin(ang)
    sa = jnp.where(half[None, :] == 0, -sin, 0.0)
    sb = jnp.where(half[None, :] == 1, sin, 0.0)
    cos = jnp.concatenate([cos, jnp.ones((l, LANES), F32)], axis=0)
    sa = jnp.concatenate([sa, jnp.zeros((l, LANES), F32)], axis=0)
    sb = jnp.concatenate([sb, jnp.zeros((l, LANES), F32)], axis=0)
    return tuple(jnp.concatenate([t * Q_SCALE, t], axis=0) for t in (cos, sa, sb))


V_ROWS = HEAD_W + HALO


SAFE_DENOM = 2.0 ** -100
BOUND_SLACK = 1.01


def _split_comps(q):
    lane = lax.broadcasted_iota(jnp.int32, q.shape, 1)
    return jnp.concatenate([jnp.where((lane // DA_HEAD_DIM) == c, q, jnp.zeros_like(q)) for c in range(2)], axis=0)


def _nt_dot(a, b):
    return lax.dot_general(a, b, (((1,), (1,)), ((), ())), preferred_element_type=F32)


def _exact_chunks(k_ref, vt_ref, q01, chunks, ck):
    def scores(j):
        return _nt_dot(k_ref[0, j * ck:(j + 1) * ck, :], q01)

    s_next = scores(chunks[0])
    m = acc = None
    for i, j in enumerate(chunks):
        s = s_next
        if i + 1 < len(chunks):
            s_next = scores(chunks[i + 1])
        mc = jnp.max(s, axis=0, keepdims=True)
        m_new = mc if m is None else jnp.maximum(m, mc)
        p = jnp.exp2(s - m_new).astype(BF16)
        pv = jnp.dot(vt_ref[:, j * ck:(j + 1) * ck], p, preferred_element_type=F32)
        acc = pv if m is None else acc * jnp.exp2(m - m_new) + pv
        m = m_new
    return acc


def _bound_chunks(k_ref, vt_ref, q01, mb, chunks, ck):
    def scores(j):
        return _nt_dot(k_ref[0, j * ck:(j + 1) * ck, :], q01)

    acc = None
    s_next = scores(chunks[0])
    for i, j in enumerate(chunks):
        s = s_next
        if i + 1 < len(chunks):
            s_next = scores(chunks[i + 1])
        p = jnp.exp2(s - mb).astype(BF16)
        pv = jnp.dot(vt_ref[:, j * ck:(j + 1) * ck], p, preferred_element_type=F32)
        acc = pv if acc is None else acc + pv
    return acc


def _attn_finish(acc, lam, hg, z, lam_init):
    t = acc.shape[1] // 2
    o = acc[0:HEAD_W] * (1.0 / acc[HEAD_W:HEAD_W + 1])
    d = o[:, 0:t] - lam * o[:, t:2 * t]
    ms = jnp.mean(d * d, axis=0, keepdims=True)
    dn = (d * lax.rsqrt(ms + NORM_EPS)).T
    on = (dn * hg) * (1.0 - lam_init)
    return (on * _silu(z.astype(F32))).astype(BF16)


def _attn_body(lam_ref, hg_ref, q_ref, qc_ref, k_ref, v_ref, z_ref, zc_ref, o_ref, vt_ref, kn_ref, *,
               lam_init, ck, sub, keep_ctx):
    qi = pl.program_id(2)
    nkv = k_ref.shape[1]
    nchunk = nkv // ck

    @pl.when(qi == 0)
    def _():
        vt_ref[0:HEAD_W, :] = v_ref[0].astype(F32).T.astype(BF16)
        vt_ref[HEAD_W:V_ROWS, :] = jnp.ones((HALO, nkv), BF16)
        kf = k_ref[0].astype(F32)
        r_i = lax.broadcasted_iota(jnp.int32, (HEAD_W, HEAD_W), 0) // DA_HEAD_DIM
        c_i = lax.broadcasted_iota(jnp.int32, (HEAD_W, HEAD_W), 1) // DA_HEAD_DIM
        sel = jnp.where(r_i == c_i, 1.0, 0.0).astype(BF16)
        kn2 = jnp.dot((kf * kf).astype(BF16), sel, preferred_element_type=F32)
        kn_ref[...] = jnp.broadcast_to(jnp.sqrt(jnp.max(kn2, axis=0, keepdims=True)), kn_ref.shape)

    lp = lam_ref[...]
    lam = (jnp.exp(jnp.sum(lp[0:1] * lp[1:2], axis=-1, keepdims=True))
           - jnp.exp(jnp.sum(lp[2:3] * lp[3:4], axis=-1, keepdims=True)) + lam_init)
    hg = hg_ref[...]

    if keep_ctx:
        @pl.when(qi == 0)
        def _():
            acc = _exact_chunks(k_ref, vt_ref, _split_comps(qc_ref[0]), [nchunk - 1], ck)
            o_ref[0, 0:qc_ref.shape[1], :] = _attn_finish(acc, lam, hg, zc_ref[0], lam_init)

    @pl.when(qi >= (1 if keep_ctx else 0))
    def _():
        nsub = q_ref.shape[1] // sub
        kn = kn_ref[0:1, :]
        kmax = jnp.concatenate([jnp.broadcast_to(kn[:, 0:1], (1, sub)),
                                jnp.broadcast_to(kn[:, DA_HEAD_DIM:DA_HEAD_DIM + 1], (1, sub))], axis=1)
        denoms = []
        for t in range(nsub):
            rows = slice(t * sub, (t + 1) * sub)
            q01 = _split_comps(q_ref[0, rows, :])
            qf = q01.astype(F32)
            qn2 = _nt_dot(jnp.ones((HALO, HEAD_W), BF16), (qf * qf).astype(BF16))[0:1]
            mb = jnp.sqrt(qn2) * kmax * BOUND_SLACK
            acc = _bound_chunks(k_ref, vt_ref, q01, mb, range(nchunk), ck)
            o_ref[0, rows, :] = _attn_finish(acc, lam, hg, z_ref[0, rows, :], lam_init)
            denoms.append(jnp.min(acc[HEAD_W:HEAD_W + 1]))
        safe = functools.reduce(jnp.minimum, denoms) >= SAFE_DENOM

        @pl.when(jnp.logical_not(safe))
        def _():
            def redo(t, carry):
                rows = pl.ds(pl.multiple_of(t * sub, sub), sub)
                acc2 = _exact_chunks(k_ref, vt_ref, _split_comps(q_ref[0, rows, :]), range(nchunk), ck)
                o_ref[0, rows, :] = _attn_finish(acc2, lam, hg, z_ref[0, rows, :], lam_init)
                return carry

            lax.fori_loop(0, nsub, redo, 0)


def _diff_attention(qkvz, p, layer_idx, s, l, keep_ctx):
    nb, r, n4 = qkvz.shape
    nh = n4 // (4 * HEAD_W)
    lam_init = 0.8 - 0.6 * math.exp(-0.3 * layer_idx)
    lamp = jnp.stack([p["lam_q1"], p["lam_k1"], p["lam_q2"], p["lam_k2"]]).astype(F32)
    lamp = jnp.pad(lamp, ((0, 4), (0, LANES - DA_HEAD_DIM)))
    hg = p["head_g"].astype(F32).reshape(1, HEAD_W)
    tq = min(Q_TILE, s)
    ck = KV_CHUNK
    sub = min(Q_SUB, tq)
    assert s % tq == 0 and tq % sub == 0 and l == ck and r % ck == 0
    nlat_q = s // tq
    n_pre = 1 if keep_ctx else 0
    ctx_blk = s // l
    kern = functools.partial(_attn_body, lam_init=lam_init, ck=ck, sub=sub, keep_ctx=keep_ctx)
    const = lambda b, h, i: (0, 0)
    lat = lambda col: pl.BlockSpec((1, tq, HEAD_W),
                                   lambda b, h, i: (b, jnp.maximum(i - n_pre, 0), col * nh + h))
    ctx = lambda col: pl.BlockSpec((1, l, HEAD_W), lambda b, h, i: (b, ctx_blk, col * nh + h))
    kv = lambda col: pl.BlockSpec((1, r, HEAD_W), lambda b, h, i: (b, 0, col * nh + h))
    out_map = lambda b, h, i: (b, jnp.where(i < n_pre, nlat_q, i - n_pre), h)
    return pl.pallas_call(
        kern,
        grid=(nb, nh, nlat_q + n_pre),
        in_specs=[pl.BlockSpec((8, LANES), const), pl.BlockSpec((1, LANES), const),
                  lat(0), ctx(0), kv(1), kv(2), lat(3), ctx(3)],
        out_specs=pl.BlockSpec((1, tq, HEAD_W), out_map),
        out_shape=jax.ShapeDtypeStruct((nb, r if keep_ctx else s, nh * HEAD_W), BF16),
        scratch_shapes=[pltpu.VMEM((V_ROWS, r), BF16), pltpu.VMEM((8, LANES), F32)],
        compiler_params=_cparams("parallel", "parallel", "arbitrary"),
        name="diff_attn",
    )(lamp, hg, qkvz, qkvz, qkvz, qkvz, qkvz, qkvz)


MXU_K = 256


def _residual_out(u, w_ref, x, tg_ref, tn_ref, g_ref, xo_ref, ho_ref, k_major=False):
    if k_major:
        y = None
        for k0 in range(0, u.shape[1], MXU_K):
            part = jnp.dot(u[:, k0:k0 + MXU_K], w_ref[k0:k0 + MXU_K, :], preferred_element_type=F32)
            y = part if y is None else y + part
    else:
        y = jnp.dot(u, w_ref[...], preferred_element_type=F32)
    xn = x + tg_ref[0, 0][2:3] * y
    xo_ref[0] = xn
    tab = tn_ref[0, 0]
    ho_ref[0] = _norm_mod(xn, g_ref[...], tab[0:1], tab[1:2]).astype(BF16)


def _out_kernel(u_ref, w_ref, x_ref, tg_ref, tn_ref, g_ref, xo_ref, ho_ref):
    _residual_out(u_ref[0], w_ref, x_ref[0], tg_ref, tn_ref, g_ref, xo_ref, ho_ref)


def _out_split_kernel(u_ref, w_ref, x_ref, c_ref, tg_ref, tn_ref, g_ref, xo_ref, ho_ref, *, nlat):
    _residual_out(u_ref[0], w_ref, _stream_tile(x_ref, c_ref, nlat), tg_ref, tn_ref, g_ref, xo_ref, ho_ref)


def _out_final_kernel(u_ref, w_ref, x_ref, tg_ref, g_ref, o_ref):
    y = jnp.dot(u_ref[0], w_ref[...], preferred_element_type=F32)
    xn = x_ref[0] + tg_ref[0, 0][2:3] * y
    ms = jnp.mean(xn * xn, axis=-1, keepdims=True)
    o_ref[0] = (xn * lax.rsqrt(ms + NORM_EPS)) * g_ref[...]


def _out_proj(u, w, x, tab, tab_next, norm_g_next, nlat):
    nb, r, dw = u.shape
    d = w.shape[1]
    row = lambda b, t: (b, t, 0)
    tabspec = pl.BlockSpec((1, 1, 8, d), lambda b, t: (b, t // nlat, 0, 0))
    if isinstance(x, tuple):
        kern, x_args, x_specs = functools.partial(_out_split_kernel, nlat=nlat), list(x), _split_stream_specs(d, nlat)
    else:
        kern, x_args, x_specs = _out_kernel, [x], [pl.BlockSpec((1, ROW_TILE, d), row)]
    return pl.pallas_call(
        kern,
        grid=(nb, r // ROW_TILE),
        in_specs=[pl.BlockSpec((1, ROW_TILE, dw), row),
                  pl.BlockSpec((dw, d), lambda b, t: (0, 0))] + x_specs + [
                  tabspec, tabspec,
                  pl.BlockSpec((1, d), lambda b, t: (0, 0))],
        out_specs=[pl.BlockSpec((1, ROW_TILE, d), row), pl.BlockSpec((1, ROW_TILE, d), row)],
        out_shape=[jax.ShapeDtypeStruct((nb, r, d), F32), jax.ShapeDtypeStruct((nb, r, d), BF16)],
        compiler_params=_cparams("parallel", "parallel"),
        name="out_proj",
    )(u, w, *x_args, tab, tab_next, norm_g_next.reshape(1, d))


def _out_proj_final(u, w, x, tab, final_g, s):
    nb, r, d = x.shape
    dw = u.shape[2]
    row = lambda b, t: (b, t, 0)
    tm = FINAL_TILE if s % FINAL_TILE == 0 else ROW_TILE
    return pl.pallas_call(
        _out_final_kernel,
        grid=(nb, s // tm),
        in_specs=[pl.BlockSpec((1, tm, dw), row),
                  pl.BlockSpec((dw, d), lambda b, t: (0, 0)),
                  pl.BlockSpec((1, tm, d), row),
                  pl.BlockSpec((1, 1, 8, d), lambda b, t: (b, 0, 0, 0)),
                  pl.BlockSpec((1, d), lambda b, t: (0, 0))],
        out_specs=pl.BlockSpec((1, tm, d), row),
        out_shape=jax.ShapeDtypeStruct((nb, s, d), F32),
        compiler_params=_cparams("parallel", "parallel"),
        name="out_proj_final",
    )(u, w, x, tab, final_g.reshape(1, d))


def _halo_valid(nlat):
    t = pl.program_id(1)
    prev_ok = jnp.logical_and(t != 0, t != nlat)
    next_ok = jnp.logical_and(t != nlat - 1, t != nlat)
    return prev_ok.astype(F32), next_ok.astype(F32)


def _conf_out_kernel(pa_ref, ca_ref, na_ref, pb_ref, cb_ref, nb_ref, z_ref, w_ref, wb_ref, lg_ref, lb_ref,
                     wo_ref, x_ref, tg_ref, tn_ref, g_ref, xo_ref, ho_ref, win_ref, u_ref, *, nlat):
    tm = ca_ref.shape[1]
    nc = win_ref.shape[0]
    pv, nv = _halo_valid(nlat)

    def glu(a_ref, b_ref):
        return a_ref[0].astype(F32) * jax.nn.sigmoid(b_ref[0].astype(F32))

    gp, gc, gn = glu(pa_ref, pb_ref) * pv, glu(ca_ref, cb_ref), glu(na_ref, nb_ref) * nv
    for c in range(nc):
        sl = slice(c * LANES, (c + 1) * LANES)
        win_ref[c, 0:HALO, :] = gp[:, sl]
        win_ref[c, HALO:HALO + tm, :] = gc[:, sl]
        win_ref[c, HALO + tm:2 * HALO + tm, :] = gn[:, sl]

    off = HALO - CF_KERNEL // 2

    def conv_chunk(c, carry):
        w = w_ref[c]
        acc = jnp.zeros((tm, LANES), F32)
        for k in range(CF_KERNEL):
            acc = acc + win_ref[c, pl.ds(off + k, tm), :] * w[k:k + 1, :]
        u_ref[c] = acc + wb_ref[c][0:1]
        return carry

    lax.fori_loop(0, nc, conv_chunk, 0)

    d = nc * LANES
    s1 = u_ref[0]
    for c in range(1, nc):
        s1 = s1 + u_ref[c]
    mu = jnp.sum(s1, axis=-1, keepdims=True) / d
    s2 = jnp.square(u_ref[0] - mu)
    for c in range(1, nc):
        s2 = s2 + jnp.square(u_ref[c] - mu)
    rs = lax.rsqrt(jnp.sum(s2, axis=-1, keepdims=True) / d + LN_EPS)
    chunks = []
    for c in range(nc):
        sl = slice(c * LANES, (c + 1) * LANES)
        y = ((u_ref[c] - mu) * rs) * lg_ref[:, sl] + lb_ref[:, sl]
        zc = z_ref[0, :, sl].astype(F32)
        chunks.append((_silu(y) * _silu(zc)).astype(BF16))
    _residual_out(jnp.concatenate(chunks, axis=1), wo_ref, x_ref[0], tg_ref, tn_ref, g_ref, xo_ref, ho_ref,
                  k_major=True)


def _halo_specs(tm, d, col, r):
    nblk = tm // HALO
    last_blk = r // HALO - 1
    prev = pl.BlockSpec((1, HALO, d), lambda b, t: (b, jnp.maximum(t * nblk - 1, 0), col))
    cur = pl.BlockSpec((1, tm, d), lambda b, t: (b, t, col))
    nxt = pl.BlockSpec((1, HALO, d), lambda b, t: (b, jnp.minimum((t + 1) * nblk, last_blk), col))
    return [prev, cur, nxt]


def _out_specs_and_args(w_out, x, tab, tab_next, norm_g_next, nlat):
    nb, r, d = x.shape
    row = lambda b, t: (b, t, 0)
    tabspec = pl.BlockSpec((1, 1, 8, d), lambda b, t: (b, t // nlat, 0, 0))
    in_specs = [pl.BlockSpec(w_out.shape, lambda b, t: (0, 0)), pl.BlockSpec((1, ROW_TILE, d), row),
                tabspec, tabspec, pl.BlockSpec((1, d), lambda b, t: (0, 0))]
    args = [w_out, x, tab, tab_next, norm_g_next.reshape(1, d)]
    out_specs = [pl.BlockSpec((1, ROW_TILE, d), row), pl.BlockSpec((1, ROW_TILE, d), row)]
    out_shape = [jax.ShapeDtypeStruct((nb, r, d), F32), jax.ShapeDtypeStruct((nb, r, d), BF16)]
    return in_specs, args, out_specs, out_shape


def _conformer_layer(abz, p, w_out, x, tab, tab_next, norm_g_next, s, l):
    nb, r, d3 = abz.shape
    d = d3 // 3
    tm = ROW_TILE
    assert l == tm and s % tm == 0 and d % LANES == 0
    nc = d // LANES
    nlat = s // tm
    specs = _halo_specs(tm, d, 0, r) + _halo_specs(tm, d, 1, r)
    z_spec = pl.BlockSpec((1, tm, d), lambda b, t: (b, t, 2))
    w = jnp.pad(p["dw_w"].astype(F32), ((0, 1), (0, 0))).reshape(CF_KERNEL + 1, nc, LANES).transpose(1, 0, 2)
    wb = jnp.broadcast_to(p["dw_b"].astype(F32).reshape(nc, 1, LANES), (nc, 8, LANES))
    full = lambda shape: pl.BlockSpec(shape, lambda b, t: (0,) * len(shape))
    o_in, o_args, out_specs, out_shape = _out_specs_and_args(w_out, x, tab, tab_next, norm_g_next, nlat)
    return pl.pallas_call(
        functools.partial(_conf_out_kernel, nlat=nlat),
        grid=(nb, r // tm),
        in_specs=specs + [z_spec, full((nc, CF_KERNEL + 1, LANES)), full((nc, 8, LANES)),
                          full((1, d)), full((1, d))] + o_in,
        out_specs=out_specs,
        out_shape=out_shape,
        scratch_shapes=[pltpu.VMEM((nc, tm + 2 * HALO, LANES), F32), pltpu.VMEM((nc, tm, LANES), F32)],
        compiler_params=_cparams("parallel", "parallel"),
        name="conformer_layer",
    )(abz, abz, abz, abz, abz, abz, abz, w, wb, p["ln_g"].astype(F32).reshape(1, d),
      p["ln_b"].astype(F32).reshape(1, d), *o_args)


def _sc_out_kernel(bg_ref, pc_ref, cc_ref, nc_ref, pv_ref, cv_ref, nv_ref, z_ref, w_ref,
                   wo_ref, x_ref, tg_ref, tn_ref, g_ref, xo_ref, ho_ref, win_ref, *, nlat):
    tm = cc_ref.shape[1]
    nc = win_ref.shape[0]
    pvalid, nvalid = _halo_valid(nlat)

    def prod(c_ref, v_ref):
        return c_ref[0].astype(F32) * v_ref[0].astype(F32)

    wp, wc, wn = prod(pc_ref, pv_ref) * pvalid, prod(cc_ref, cv_ref), prod(nc_ref, nv_ref) * nvalid
    off = HALO - SC_KERNEL // 2
    w = w_ref[...]
    chunks = []
    for c in range(nc):
        sl = slice(c * LANES, (c + 1) * LANES)
        win_ref[c, 0:HALO, :] = wp[:, sl]
        win_ref[c, HALO:HALO + tm, :] = wc[:, sl]
        win_ref[c, HALO + tm:2 * HALO + tm, :] = wn[:, sl]
        y = win_ref[c, pl.ds(off, tm), :] * w[0:1, sl]
        for k in range(1, SC_KERNEL):
            y = y + win_ref[c, pl.ds(off + k, tm), :] * w[k:k + 1, sl]
        chunks.append(((bg_ref[0, :, sl].astype(F32) * y) * _silu(z_ref[0, :, sl].astype(F32))).astype(BF16))
    u = jnp.concatenate(chunks, axis=1)
    _residual_out(u, wo_ref, x_ref[0], tg_ref, tn_ref, g_ref, xo_ref, ho_ref, k_major=True)


def _shortconv_layer(bcvz, p, w_out, x, tab, tab_next, norm_g_next, s, l):
    nb, r, d4 = bcvz.shape
    d = d4 // 4
    tm = ROW_TILE
    assert l == tm and s % tm == 0
    nlat = s // tm
    specs = [pl.BlockSpec((1, tm, d), lambda b, t: (b, t, 0))]
    specs += _halo_specs(tm, d, 1, r) + _halo_specs(tm, d, 2, r)
    specs.append(pl.BlockSpec((1, tm, d), lambda b, t: (b, t, 3)))
    specs.append(pl.BlockSpec((8, d), lambda b, t: (0, 0)))
    w = jnp.pad(p["conv_w"].astype(F32), ((0, 8 - SC_KERNEL), (0, 0)))
    o_in, o_args, out_specs, out_shape = _out_specs_and_args(w_out, x, tab, tab_next, norm_g_next, nlat)
    return pl.pallas_call(
        functools.partial(_sc_out_kernel, nlat=nlat),
        grid=(nb, r // tm),
        in_specs=specs + o_in,
        out_specs=out_specs,
        out_shape=out_shape,
        scratch_shapes=[pltpu.VMEM((d // LANES, tm + 2 * HALO, LANES), F32)],
        compiler_params=_cparams("parallel", "parallel"),
        name="shortconv_layer",
    )(bcvz, bcvz, bcvz, bcvz, bcvz, bcvz, bcvz, bcvz, w, *o_args)


def _forward(x, c, ctx, c_ctx, layers, final_norm_g):
    nb, s, d = x.shape
    l = ctx.shape[1]
    r = s + l
    assert l == ROW_TILE and s % ROW_TILE == 0 and nb < 8
    nlat = s // ROW_TILE
    depth = len(layers)
    kinds = ("attn", "conformer", "shortconv")

    cond8 = jnp.zeros((8, d), F32).at[:nb].set(c).at[nb].set(c_ctx)
    tabs = [_mod_table(_adaln(cond8, p["ada_w"], p["ada_b"]), nb, d) for p in layers]
    rope = _rope_tables(s, l)

    h = _prologue(x, ctx, tabs[0], layers[0]["norm_g"])
    xs = (x, ctx)
    for i, p in enumerate(layers):
        kind = kinds[i % len(kinds)]
        last = i == depth - 1
        w_in = p["w_in"]
        w_out = p["w_out"].astype(BF16)
        h2 = h.reshape(nb * r, d)
        if kind == "attn":
            qkvz = _in_proj(h2, w_in, r, rope=rope, rope_cols=w_in.shape[1] // 2).reshape(nb, r, -1)
            u = _diff_attention(qkvz, p, i, s, l, keep_ctx=not last)
            if last:
                return _out_proj_final(u, w_out, xs, tabs[i], final_norm_g, s)
            xs, h = _out_proj(u, w_out, xs, tabs[i], tabs[i + 1], layers[i + 1]["norm_g"], nlat)
        else:
            assert not last and i > 0, "the stack starts and ends with an attention layer"
            proj = _in_proj(h2, w_in, r).reshape(nb, r, -1)
            layer = _conformer_layer if kind == "conformer" else _shortconv_layer
            xs, h = layer(proj, p, w_out, xs, tabs[i], tabs[i + 1], layers[i + 1]["norm_g"], s, l)


def kernel(x, c, ctx, c_ctx, l0_norm_g, l0_ada_w, l0_ada_b, l0_w_in, l0_lam_q1, l0_lam_k1, l0_lam_q2, l0_lam_k2, l0_head_g, l0_w_out, l1_norm_g, l1_ada_w, l1_ada_b, l1_w_in, l1_dw_w, l1_dw_b, l1_ln_g, l1_ln_b, l1_w_out, l2_norm_g, l2_ada_w, l2_ada_b, l2_w_in, l2_conv_w, l2_w_out, l3_norm_g, l3_ada_w, l3_ada_b, l3_w_in, l3_lam_q1, l3_lam_k1, l3_lam_q2, l3_lam_k2, l3_head_g, l3_w_out, final_norm_g):
    layers = [
        dict(norm_g=l0_norm_g, ada_w=l0_ada_w, ada_b=l0_ada_b, w_in=l0_w_in, lam_q1=l0_lam_q1, lam_k1=l0_lam_k1,
             lam_q2=l0_lam_q2, lam_k2=l0_lam_k2, head_g=l0_head_g, w_out=l0_w_out),
        dict(norm_g=l1_norm_g, ada_w=l1_ada_w, ada_b=l1_ada_b, w_in=l1_w_in, dw_w=l1_dw_w, dw_b=l1_dw_b,
             ln_g=l1_ln_g, ln_b=l1_ln_b, w_out=l1_w_out),
        dict(norm_g=l2_norm_g, ada_w=l2_ada_w, ada_b=l2_ada_b, w_in=l2_w_in, conv_w=l2_conv_w, w_out=l2_w_out),
        dict(norm_g=l3_norm_g, ada_w=l3_ada_w, ada_b=l3_ada_b, w_in=l3_w_in, lam_q1=l3_lam_q1, lam_k1=l3_lam_k1,
             lam_q2=l3_lam_q2, lam_k2=l3_lam_k2, head_g=l3_head_g, w_out=l3_w_out),
    ]
    return _forward(x, c, ctx, c_ctx, layers, final_norm_g)
```

```python
import functools
import math

import jax
import jax.numpy as jnp
from jax import lax
from jax.experimental import pallas as pl
from jax.experimental.pallas import tpu as pltpu

F32 = jnp.float32
BF16 = jnp.bfloat16

LANES = 128
HALO = 16
ROW_TILE = 256
DA_HEAD_DIM = 64
HEAD_W = 2 * DA_HEAD_DIM
ROPE_BASE = 10000.0
ROPE_FREQS = DA_HEAD_DIM // 4
GRID_W = 64
CF_KERNEL = 31
SC_KERNEL = 3
NORM_EPS = 1e-6
LN_EPS = 1e-5
Q_TILE = 2048
Q_SUB = 512
FINAL_TILE = 512
KV_CHUNK = 256
Q_SCALE = math.log2(math.e) / math.sqrt(DA_HEAD_DIM)
VMEM_LIMIT = 52 * 1024 * 1024


def _cparams(*sem):
    return pltpu.CompilerParams(dimension_semantics=sem, vmem_limit_bytes=VMEM_LIMIT)


def _silu(x):
    return x * jax.nn.sigmoid(x)


def _ada_kernel(c_ref, w_ref, b_ref, o_ref):
    a = _silu(c_ref[...]).astype(BF16)
    o_ref[...] = jnp.dot(a, w_ref[...].astype(BF16), preferred_element_type=F32) + b_ref[...]


def _adaln(cond8, w, b):
    d, n = w.shape
    tn = min(1024, n)
    return pl.pallas_call(
        _ada_kernel,
        grid=(n // tn,),
        in_specs=[pl.BlockSpec((8, d), lambda j: (0, 0)),
                  pl.BlockSpec((d, tn), lambda j: (0, j)),
                  pl.BlockSpec((1, tn), lambda j: (0, j))],
        out_specs=pl.BlockSpec((8, tn), lambda j: (0, j)),
        out_shape=jax.ShapeDtypeStruct((8, n), F32),
        compiler_params=_cparams("parallel"),
        name="adaln",
    )(cond8, w, b.reshape(1, n))


def _mod_table(m, nb, d):
    sh, sc, g = m[:, :d], m[:, d:2 * d], m[:, 2 * d:]
    rows = jnp.stack([1.0 + sc, sh, g], axis=1)
    lat = rows[:nb]
    ctx = jnp.broadcast_to(rows[nb][None], (nb, 3, d))
    tab = jnp.stack([lat, ctx], axis=1)
    return jnp.pad(tab, ((0, 0), (0, 0), (0, 5), (0, 0)))


def _norm_mod(x, g, scale1p, shift):
    ms = jnp.mean(x * x, axis=-1, keepdims=True)
    y = x * lax.rsqrt(ms + NORM_EPS)
    return (y * g) * scale1p + shift


def _stream_tile(x_ref, c_ref, nlat):
    return jnp.where(pl.program_id(1) < nlat, x_ref[0], c_ref[0])


def _split_stream_specs(d, nlat):
    return [pl.BlockSpec((1, ROW_TILE, d), lambda b, t: (b, jnp.minimum(t, nlat - 1), 0)),
            pl.BlockSpec((1, ROW_TILE, d), lambda b, t: (b, 0, 0))]


def _pro_kernel(x_ref, c_ref, tab_ref, g_ref, h_ref, *, nlat):
    tab = tab_ref[0, 0]
    h_ref[0] = _norm_mod(_stream_tile(x_ref, c_ref, nlat), g_ref[...], tab[0:1], tab[1:2]).astype(BF16)


def _prologue(x, ctx, tab, norm_g):
    nb, s, d = x.shape
    l = ctx.shape[1]
    r = s + l
    nlat = s // ROW_TILE
    row = lambda b, t: (b, t, 0)
    return pl.pallas_call(
        functools.partial(_pro_kernel, nlat=nlat),
        grid=(nb, r // ROW_TILE),
        in_specs=_split_stream_specs(d, nlat) + [
            pl.BlockSpec((1, 1, 8, d), lambda b, t: (b, t // nlat, 0, 0)),
            pl.BlockSpec((1, d), lambda b, t: (0, 0))],
        out_specs=pl.BlockSpec((1, ROW_TILE, d), row),
        out_shape=jax.ShapeDtypeStruct((nb, r, d), BF16),
        compiler_params=_cparams("parallel", "arbitrary"),
        name="prologue",
    )(x, ctx, tab, norm_g.reshape(1, d))


def _in_kernel(h_ref, w_ref, o_ref, wb_ref):
    @pl.when(pl.program_id(1) == 0)
    def _():
        wb_ref[...] = w_ref[...].astype(BF16)

    o_ref[...] = jnp.dot(h_ref[...], wb_ref[...], preferred_element_type=F32).astype(BF16)


def _in_rope_kernel(h_ref, w_ref, cos_ref, sa_ref, sb_ref, o_ref, wb_ref, *, n_rope, tn, sub):
    j = pl.program_id(0)
    tm = h_ref.shape[0]

    @pl.when(pl.program_id(1) == 0)
    def _():
        wb_ref[...] = w_ref[...].astype(BF16)

    @pl.when(j < n_rope)
    def _():
        for r0 in range(0, tm, sub):
            rows = slice(r0, r0 + sub)
            acc = jnp.dot(h_ref[rows, :], wb_ref[...], preferred_element_type=F32)
            cos, sa, sb = cos_ref[rows, :], sa_ref[rows, :], sb_ref[rows, :]
            for c in range(tn // LANES):
                xc = acc[:, c * LANES:(c + 1) * LANES]
                rot = (xc * cos + pltpu.roll(xc, LANES - ROPE_FREQS, 1) * sa
                       + pltpu.roll(xc, ROPE_FREQS, 1) * sb)
                o_ref[rows, c * LANES:(c + 1) * LANES] = rot.astype(BF16)

    @pl.when(j >= n_rope)
    def _():
        o_ref[...] = jnp.dot(h_ref[...], wb_ref[...], preferred_element_type=F32).astype(BF16)


def _in_tiles(t, r, d, n):
    tm = r // 4
    tn = 1024 if d >= 1024 else d
    assert tm % HALO == 0 and t % tm == 0 and n % tn == 0
    return tm, tn


def _in_proj(h, w, r, rope=None, rope_cols=0):
    t, d = h.shape
    n = w.shape[1]
    tm, tn = _in_tiles(t, r, d, n)
    grid = (n // tn, t // tm)
    h_spec = pl.BlockSpec((tm, d), lambda j, i: (i, 0))
    w_spec = pl.BlockSpec((d, tn), lambda j, i: (0, j))
    o_spec = pl.BlockSpec((tm, tn), lambda j, i: (i, j))
    out_shape = jax.ShapeDtypeStruct((t, n), BF16)
    scratch = [pltpu.VMEM((d, tn), BF16)]
    if rope is None:
        return pl.pallas_call(_in_kernel, grid=grid, in_specs=[h_spec, w_spec], out_specs=o_spec,
                              out_shape=out_shape, scratch_shapes=scratch,
                              compiler_params=_cparams("parallel", "arbitrary"), name="in_proj")(h, w)
    assert rope_cols % (2 * tn) == 0 and tn % LANES == 0
    per = r // tm
    n_q = rope_cols // (2 * tn)
    t_spec = pl.BlockSpec((tm, LANES), lambda j, i: (i % per + jnp.where(j >= n_q, per, 0), 0))
    sub = tm // 4 if (tm // 4) % HALO == 0 else tm
    kern = functools.partial(_in_rope_kernel, n_rope=rope_cols // tn, tn=tn, sub=sub)
    return pl.pallas_call(kern, grid=grid, in_specs=[h_spec, w_spec, t_spec, t_spec, t_spec],
                          out_specs=o_spec, out_shape=out_shape, scratch_shapes=scratch,
                          compiler_params=_cparams("parallel", "arbitrary"),
                          name="in_proj_rope")(h, w, *rope)


def _rope_tables(s, l):
    t = jnp.arange(s)
    row = (t // GRID_W).astype(F32)
    col = (t % GRID_W).astype(F32)
    inv = ROPE_BASE ** (-jnp.arange(ROPE_FREQS, dtype=F32) / ROPE_FREQS)
    lane = jnp.arange(LANES)
    dd = lane % DA_HEAD_DIM
    axis, half, f = dd // (2 * ROPE_FREQS), (dd % (2 * ROPE_FREQS)) // ROPE_FREQS, dd % ROPE_FREQS
    pos = jnp.where(axis[None, :] == 0, row[:, None], col[:, None])
    ang = pos * inv[f][None, :]
    cos, sin = jnp.cos(ang), jnp.sin(ang)
    sa = jnp.where(half[None, :] == 0, -sin, 0.0)
    sb = jnp.where(half[None, :] == 1, sin, 0.0)
    cos = jnp.concatenate([cos, jnp.ones((l, LANES), F32)], axis=0)
    sa = jnp.concatenate([sa, jnp.zeros((l, LANES), F32)], axis=0)
    sb = jnp.concatenate([sb, jnp.zeros((l, LANES), F32)], axis=0)
    return tuple(jnp.concatenate([t * Q_SCALE, t], axis=0) for t in (cos, sa, sb))


V_ROWS = HEAD_W + HALO


SAFE_DENOM = 2.0 ** -100
BOUND_SLACK = 1.01


def _split_comps(q):
    lane = lax.broadcasted_iota(jnp.int32, q.shape, 1)
    return jnp.concatenate([jnp.where((lane // DA_HEAD_DIM) == c, q, jnp.zeros_like(q)) for c in range(2)], axis=0)


def _nt_dot(a, b):
    return lax.dot_general(a, b, (((1,), (1,)), ((), ())), preferred_element_type=F32)


def _exact_chunks(k_ref, vt_ref, q01, chunks, ck):
    def scores(j):
        return _nt_dot(k_ref[0, j * ck:(j + 1) * ck, :], q01)

    s_next = scores(chunks[0])
    m = acc = None
    for i, j in enumerate(chunks):
        s = s_next
        if i + 1 < len(chunks):
            s_next = scores(chunks[i + 1])
        mc = jnp.max(s, axis=0, keepdims=True)
        m_new = mc if m is None else jnp.maximum(m, mc)
        p = jnp.exp2(s - m_new).astype(BF16)
        pv = jnp.dot(vt_ref[:, j * ck:(j + 1) * ck], p, preferred_element_type=F32)
        acc = pv if m is None else acc * jnp.exp2(m - m_new) + pv
        m = m_new
    return acc


def _bound_chunks(k_ref, vt_ref, q01, mb, chunks, ck):
    def scores(j):
        return _nt_dot(k_ref[0, j * ck:(j + 1) * ck, :], q01)

    acc = None
    s_next = scores(chunks[0])
    for i, j in enumerate(chunks):
        s = s_next
        if i + 1 < len(chunks):
            s_next = scores(chunks[i + 1])
        p = jnp.exp2(s - mb).astype(BF16)
        pv = jnp.dot(vt_ref[:, j * ck:(j + 1) * ck], p, preferred_element_type=F32)
        acc = pv if acc is None else acc + pv
    return acc


def _attn_finish(acc, lam, hg, z, lam_init):
    t = acc.shape[1] // 2
    o = acc[0:HEAD_W] * (1.0 / acc[HEAD_W:HEAD_W + 1])
    d = o[:, 0:t] - lam * o[:, t:2 * t]
    ms = jnp.mean(d * d, axis=0, keepdims=True)
    dn = (d * lax.rsqrt(ms + NORM_EPS)).T
    on = (dn * hg) * (1.0 - lam_init)
    return (on * _silu(z.astype(F32))).astype(BF16)


def _attn_body(lam_ref, hg_ref, q_ref, qc_ref, k_ref, v_ref, z_ref, zc_ref, o_ref, vt_ref, kn_ref, *,
               lam_init, ck, sub, keep_ctx):
    qi = pl.program_id(2)
    nkv = k_ref.shape[1]
    nchunk = nkv // ck

    @pl.when(qi == 0)
    def _():
        vt_ref[0:HEAD_W, :] = v_ref[0].astype(F32).T.astype(BF16)
        vt_ref[HEAD_W:V_ROWS, :] = jnp.ones((HALO, nkv), BF16)
        kf = k_ref[0].astype(F32)
        r_i = lax.broadcasted_iota(jnp.int32, (HEAD_W, HEAD_W), 0) // DA_HEAD_DIM
        c_i = lax.broadcasted_iota(jnp.int32, (HEAD_W, HEAD_W), 1) // DA_HEAD_DIM
        sel = jnp.where(r_i == c_i, 1.0, 0.0).astype(BF16)
        kn2 = jnp.dot((kf * kf).astype(BF16), sel, preferred_element_type=F32)
        kn_ref[...] = jnp.broadcast_to(jnp.sqrt(jnp.max(kn2, axis=0, keepdims=True)), kn_ref.shape)

    lp = lam_ref[...]
    lam = (jnp.exp(jnp.sum(lp[0:1] * lp[1:2], axis=-1, keepdims=True))
           - jnp.exp(jnp.sum(lp[2:3] * lp[3:4], axis=-1, keepdims=True)) + lam_init)
    hg = hg_ref[...]

    if keep_ctx:
        @pl.when(qi == 0)
        def _():
            acc = _exact_chunks(k_ref, vt_ref, _split_comps(qc_ref[0]), [nchunk - 1], ck)
            o_ref[0, 0:qc_ref.shape[1], :] = _attn_finish(acc, lam, hg, zc_ref[0], lam_init)

    @pl.when(qi >= (1 if keep_ctx else 0))
    def _():
        nsub = q_ref.shape[1] // sub
        kn = kn_ref[0:1, :]
        kmax = jnp.concatenate([jnp.broadcast_to(kn[:, 0:1], (1, sub)),
                                jnp.broadcast_to(kn[:, DA_HEAD_DIM:DA_HEAD_DIM + 1], (1, sub))], axis=1)
        denoms = []
        for t in range(nsub):
            rows = slice(t * sub, (t + 1) * sub)
            q01 = _split_comps(q_ref[0, rows, :])
            qf = q01.astype(F32)
            qn2 = _nt_dot(jnp.ones((HALO, HEAD_W), BF16), (qf * qf).astype(BF16))[0:1]
            mb = jnp.sqrt(qn2) * kmax * BOUND_SLACK
            acc = _bound_chunks(k_ref, vt_ref, q01, mb, range(nchunk), ck)
            o_ref[0, rows, :] = _attn_finish(acc, lam, hg, z_ref[0, rows, :], lam_init)
            denoms.append(jnp.min(acc[HEAD_W:HEAD_W + 1]))
        safe = functools.reduce(jnp.minimum, denoms) >= SAFE_DENOM

        @pl.when(jnp.logical_not(safe))
        def _():
            def redo(t, carry):
                rows = pl.ds(pl.multiple_of(t * sub, sub), sub)
                acc2 = _exact_chunks(k_ref, vt_ref, _split_comps(q_ref[0, rows, :]), range(nchunk), ck)
                o_ref[0, rows, :] = _attn_finish(acc2, lam, hg, z_ref[0, rows, :], lam_init)
                return carry

            lax.fori_loop(0, nsub, redo, 0)


def _diff_attention(qkvz, p, layer_idx, s, l, keep_ctx):
    nb, r, n4 = qkvz.shape
    nh = n4 // (4 * HEAD_W)
    lam_init = 0.8 - 0.6 * math.exp(-0.3 * layer_idx)
    lamp = jnp.stack([p["lam_q1"], p["lam_k1"], p["lam_q2"], p["lam_k2"]]).astype(F32)
    lamp = jnp.pad(lamp, ((0, 4), (0, LANES - DA_HEAD_DIM)))
    hg = p["head_g"].astype(F32).reshape(1, HEAD_W)
    tq = min(Q_TILE, s)
    ck = KV_CHUNK
    sub = min(Q_SUB, tq)
    assert s % tq == 0 and tq % sub == 0 and l == ck and r % ck == 0
    nlat_q = s // tq
    n_pre = 1 if keep_ctx else 0
    ctx_blk = s // l
    kern = functools.partial(_attn_body, lam_init=lam_init, ck=ck, sub=sub, keep_ctx=keep_ctx)
    const = lambda b, h, i: (0, 0)
    lat = lambda col: pl.BlockSpec((1, tq, HEAD_W),
                                   lambda b, h, i: (b, jnp.maximum(i - n_pre, 0), col * nh + h))
    ctx = lambda col: pl.BlockSpec((1, l, HEAD_W), lambda b, h, i: (b, ctx_blk, col * nh + h))
    kv = lambda col: pl.BlockSpec((1, r, HEAD_W), lambda b, h, i: (b, 0, col * nh + h))
    out_map = lambda b, h, i: (b, jnp.where(i < n_pre, nlat_q, i - n_pre), h)
    return pl.pallas_call(
        kern,
        grid=(nb, nh, nlat_q + n_pre),
        in_specs=[pl.BlockSpec((8, LANES), const), pl.BlockSpec((1, LANES), const),
                  lat(0), ctx(0), kv(1), kv(2), lat(3), ctx(3)],
        out_specs=pl.BlockSpec((1, tq, HEAD_W), out_map),
        out_shape=jax.ShapeDtypeStruct((nb, r if keep_ctx else s, nh * HEAD_W), BF16),
        scratch_shapes=[pltpu.VMEM((V_ROWS, r), BF16), pltpu.VMEM((8, LANES), F32)],
        compiler_params=_cparams("parallel", "parallel", "arbitrary"),
        name="diff_attn",
    )(lamp, hg, qkvz, qkvz, qkvz, qkvz, qkvz, qkvz)


MXU_K = 256


def _residual_out(u, w_ref, x, tg_ref, tn_ref, g_ref, xo_ref, ho_ref, k_major=False):
    if k_major:
        y = None
        for k0 in range(0, u.shape[1], MXU_K):
            part = jnp.dot(u[:, k0:k0 + MXU_K], w_ref[k0:k0 + MXU_K, :], preferred_element_type=F32)
            y = part if y is None else y + part
    else:
        y = jnp.dot(u, w_ref[...], preferred_element_type=F32)
    xn = x + tg_ref[0, 0][2:3] * y
    xo_ref[0] = xn
    tab = tn_ref[0, 0]
    ho_ref[0] = _norm_mod(xn, g_ref[...], tab[0:1], tab[1:2]).astype(BF16)


def _out_kernel(u_ref, w_ref, x_ref, tg_ref, tn_ref, g_ref, xo_ref, ho_ref):
    _residual_out(u_ref[0], w_ref, x_ref[0], tg_ref, tn_ref, g_ref, xo_ref, ho_ref)


def _out_split_kernel(u_ref, w_ref, x_ref, c_ref, tg_ref, tn_ref, g_ref, xo_ref, ho_ref, *, nlat):
    _residual_out(u_ref[0], w_ref, _stream_tile(x_ref, c_ref, nlat), tg_ref, tn_ref, g_ref, xo_ref, ho_ref)


def _out_final_kernel(u_ref, w_ref, x_ref, tg_ref, g_ref, o_ref):
    y = jnp.dot(u_ref[0], w_ref[...], preferred_element_type=F32)
    xn = x_ref[0] + tg_ref[0, 0][2:3] * y
    ms = jnp.mean(xn * xn, axis=-1, keepdims=True)
    o_ref[0] = (xn * lax.rsqrt(ms + NORM_EPS)) * g_ref[...]


def _out_proj(u, w, x, tab, tab_next, norm_g_next, nlat):
    nb, r, dw = u.shape
    d = w.shape[1]
    row = lambda b, t: (b, t, 0)
    tabspec = pl.BlockSpec((1, 1, 8, d), lambda b, t: (b, t // nlat, 0, 0))
    if isinstance(x, tuple):
        kern, x_args, x_specs = functools.partial(_out_split_kernel, nlat=nlat), list(x), _split_stream_specs(d, nlat)
    else:
        kern, x_args, x_specs = _out_kernel, [x], [pl.BlockSpec((1, ROW_TILE, d), row)]
    return pl.pallas_call(
        kern,
        grid=(nb, r // ROW_TILE),
        in_specs=[pl.BlockSpec((1, ROW_TILE, dw), row),
                  pl.BlockSpec((dw, d), lambda b, t: (0, 0))] + x_specs + [
                  tabspec, tabspec,
                  pl.BlockSpec((1, d), lambda b, t: (0, 0))],
        out_specs=[pl.BlockSpec((1, ROW_TILE, d), row), pl.BlockSpec((1, ROW_TILE, d), row)],
        out_shape=[jax.ShapeDtypeStruct((nb, r, d), F32), jax.ShapeDtypeStruct((nb, r, d), BF16)],
        compiler_params=_cparams("parallel", "parallel"),
        name="out_proj",
    )(u, w, *x_args, tab, tab_next, norm_g_next.reshape(1, d))


def _out_proj_final(u, w, x, tab, final_g, s):
    nb, r, d = x.shape
    dw = u.shape[2]
    row = lambda b, t: (b, t, 0)
    tm = FINAL_TILE if s % FINAL_TILE == 0 else ROW_TILE
    return pl.pallas_call(
        _out_final_kernel,
        grid=(nb, s // tm),
        in_specs=[pl.BlockSpec((1, tm, dw), row),
                  pl.BlockSpec((dw, d), lambda b, t: (0, 0)),
                  pl.BlockSpec((1, tm, d), row),
                  pl.BlockSpec((1, 1, 8, d), lambda b, t: (b, 0, 0, 0)),
                  pl.BlockSpec((1, d), lambda b, t: (0, 0))],
        out_specs=pl.BlockSpec((1, tm, d), row),
        out_shape=jax.ShapeDtypeStruct((nb, s, d), F32),
        compiler_params=_cparams("parallel", "parallel"),
        name="out_proj_final",
    )(u, w, x, tab, final_g.reshape(1, d))


def _halo_valid(nlat):
    t = pl.program_id(1)
    prev_ok = jnp.logical_and(t != 0, t != nlat)
    next_ok = jnp.logical_and(t != nlat - 1, t != nlat)
    return prev_ok.astype(F32), next_ok.astype(F32)


def _conf_out_kernel(pa_ref, ca_ref, na_ref, pb_ref, cb_ref, nb_ref, z_ref, w_ref, wb_ref, lg_ref, lb_ref,
                     wo_ref, x_ref, tg_ref, tn_ref, g_ref, xo_ref, ho_ref, win_ref, u_ref, *, nlat):
    tm = ca_ref.shape[1]
    nc = win_ref.shape[0]
    pv, nv = _halo_valid(nlat)

    def glu(a_ref, b_ref):
        return a_ref[0].astype(F32) * jax.nn.sigmoid(b_ref[0].astype(F32))

    gp, gc, gn = glu(pa_ref, pb_ref) * pv, glu(ca_ref, cb_ref), glu(na_ref, nb_ref) * nv
    for c in range(nc):
        sl = slice(c * LANES, (c + 1) * LANES)
        win_ref[c, 0:HALO, :] = gp[:, sl]
        win_ref[c, HALO:HALO + tm, :] = gc[:, sl]
        win_ref[c, HALO + tm:2 * HALO + tm, :] = gn[:, sl]

    off = HALO - CF_KERNEL // 2

    def conv_chunk(c, carry):
        w = w_ref[c]
        acc = jnp.zeros((tm, LANES), F32)
        for k in range(CF_KERNEL):
            acc = acc + win_ref[c, pl.ds(off + k, tm), :] * w[k:k + 1, :]
        u_ref[c] = acc + wb_ref[c][0:1]
        return carry

    lax.fori_loop(0, nc, conv_chunk, 0)

    d = nc * LANES
    s1 = u_ref[0]
    for c in range(1, nc):
        s1 = s1 + u_ref[c]
    mu = jnp.sum(s1, axis=-1, keepdims=True) / d
    s2 = jnp.square(u_ref[0] - mu)
    for c in range(1, nc):
        s2 = s2 + jnp.square(u_ref[c] - mu)
    rs = lax.rsqrt(jnp.sum(s2, axis=-1, keepdims=True) / d + LN_EPS)
    chunks = []
    for c in range(nc):
        sl = slice(c * LANES, (c + 1) * LANES)
        y = ((u_ref[c] - mu) * rs) * lg_ref[:, sl] + lb_ref[:, sl]
        zc = z_ref[0, :, sl].astype(F32)
        chunks.append((_silu(y) * _silu(zc)).astype(BF16))
    _residual_out(jnp.concatenate(chunks, axis=1), wo_ref, x_ref[0], tg_ref, tn_ref, g_ref, xo_ref, ho_ref,
                  k_major=True)


def _halo_specs(tm, d, col, r):
    nblk = tm // HALO
    last_blk = r // HALO - 1
    prev = pl.BlockSpec((1, HALO, d), lambda b, t: (b, jnp.maximum(t * nblk - 1, 0), col))
    cur = pl.BlockSpec((1, tm, d), lambda b, t: (b, t, col))
    nxt = pl.BlockSpec((1, HALO, d), lambda b, t: (b, jnp.minimum((t + 1) * nblk, last_blk), col))
    return [prev, cur, nxt]


def _out_specs_and_args(w_out, x, tab, tab_next, norm_g_next, nlat):
    nb, r, d = x.shape
    row = lambda b, t: (b, t, 0)
    tabspec = pl.BlockSpec((1, 1, 8, d), lambda b, t: (b, t // nlat, 0, 0))
    in_specs = [pl.BlockSpec(w_out.shape, lambda b, t: (0, 0)), pl.BlockSpec((1, ROW_TILE, d), row),
                tabspec, tabspec, pl.BlockSpec((1, d), lambda b, t: (0, 0))]
    args = [w_out, x, tab, tab_next, norm_g_next.reshape(1, d)]
    out_specs = [pl.BlockSpec((1, ROW_TILE, d), row), pl.BlockSpec((1, ROW_TILE, d), row)]
    out_shape = [jax.ShapeDtypeStruct((nb, r, d), F32), jax.ShapeDtypeStruct((nb, r, d), BF16)]
    return in_specs, args, out_specs, out_shape


def _conformer_layer(abz, p, w_out, x, tab, tab_next, norm_g_next, s, l):
    nb, r, d3 = abz.shape
    d = d3 // 3
    tm = ROW_TILE
    assert l == tm and s % tm == 0 and d % LANES == 0
    nc = d // LANES
    nlat = s // tm
    specs = _halo_specs(tm, d, 0, r) + _halo_specs(tm, d, 1, r)
    z_spec = pl.BlockSpec((1, tm, d), lambda b, t: (b, t, 2))
    w = jnp.pad(p["dw_w"].astype(F32), ((0, 1), (0, 0))).reshape(CF_KERNEL + 1, nc, LANES).transpose(1, 0, 2)
    wb = jnp.broadcast_to(p["dw_b"].astype(F32).reshape(nc, 1, LANES), (nc, 8, LANES))
    full = lambda shape: pl.BlockSpec(shape, lambda b, t: (0,) * len(shape))
    o_in, o_args, out_specs, out_shape = _out_specs_and_args(w_out, x, tab, tab_next, norm_g_next, nlat)
    return pl.pallas_call(
        functools.partial(_conf_out_kernel, nlat=nlat),
        grid=(nb, r // tm),
        in_specs=specs + [z_spec, full((nc, CF_KERNEL + 1, LANES)), full((nc, 8, LANES)),
                          full((1, d)), full((1, d))] + o_in,
        out_specs=out_specs,
        out_shape=out_shape,
        scratch_shapes=[pltpu.VMEM((nc, tm + 2 * HALO, LANES), F32), pltpu.VMEM((nc, tm, LANES), F32)],
        compiler_params=_cparams("parallel", "parallel"),
        name="conformer_layer",
    )(abz, abz, abz, abz, abz, abz, abz, w, wb, p["ln_g"].astype(F32).reshape(1, d),
      p["ln_b"].astype(F32).reshape(1, d), *o_args)


def _sc_out_kernel(bg_ref, pc_ref, cc_ref, nc_ref, pv_ref, cv_ref, nv_ref, z_ref, w_ref,
                   wo_ref, x_ref, tg_ref, tn_ref, g_ref, xo_ref, ho_ref, win_ref, *, nlat):
    tm = cc_ref.shape[1]
    nc = win_ref.shape[0]
    pvalid, nvalid = _halo_valid(nlat)

    def prod(c_ref, v_ref):
        return c_ref[0].astype(F32) * v_ref[0].astype(F32)

    wp, wc, wn = prod(pc_ref, pv_ref) * pvalid, prod(cc_ref, cv_ref), prod(nc_ref, nv_ref) * nvalid
    off = HALO - SC_KERNEL // 2
    w = w_ref[...]
    chunks = []
    for c in range(nc):
        sl = slice(c * LANES, (c + 1) * LANES)
        win_ref[c, 0:HALO, :] = wp[:, sl]
        win_ref[c, HALO:HALO + tm, :] = wc[:, sl]
        win_ref[c, HALO + tm:2 * HALO + tm, :] = wn[:, sl]
        y = win_ref[c, pl.ds(off, tm), :] * w[0:1, sl]
        for k in range(1, SC_KERNEL):
            y = y + win_ref[c, pl.ds(off + k, tm), :] * w[k:k + 1, sl]
        chunks.append(((bg_ref[0, :, sl].astype(F32) * y) * _silu(z_ref[0, :, sl].astype(F32))).astype(BF16))
    u = jnp.concatenate(chunks, axis=1)
    _residual_out(u, wo_ref, x_ref[0], tg_ref, tn_ref, g_ref, xo_ref, ho_ref, k_major=True)


def _shortconv_layer(bcvz, p, w_out, x, tab, tab_next, norm_g_next, s, l):
    nb, r, d4 = bcvz.shape
    d = d4 // 4
    tm = ROW_TILE
    assert l == tm and s % tm == 0
    nlat = s // tm
    specs = [pl.BlockSpec((1, tm, d), lambda b, t: (b, t, 0))]
    specs += _halo_specs(tm, d, 1, r) + _halo_specs(tm, d, 2, r)
    specs.append(pl.BlockSpec((1, tm, d), lambda b, t: (b, t, 3)))
    specs.append(pl.BlockSpec((8, d), lambda b, t: (0, 0)))
    w = jnp.pad(p["conv_w"].astype(F32), ((0, 8 - SC_KERNEL), (0, 0)))
    o_in, o_args, out_specs, out_shape = _out_specs_and_args(w_out, x, tab, tab_next, norm_g_next, nlat)
    return pl.pallas_call(
        functools.partial(_sc_out_kernel, nlat=nlat),
        grid=(nb, r // tm),
        in_specs=specs + o_in,
        out_specs=out_specs,
        out_shape=out_shape,
        scratch_shapes=[pltpu.VMEM((d // LANES, tm + 2 * HALO, LANES), F32)],
        compiler_params=_cparams("parallel", "parallel"),
        name="shortconv_layer",
    )(bcvz, bcvz, bcvz, bcvz, bcvz, bcvz, bcvz, bcvz, w, *o_args)


def _forward(x, c, ctx, c_ctx, layers, final_norm_g):
    nb, s, d = x.shape
    l = ctx.shape[1]
    r = s + l
    assert l == ROW_TILE and s % ROW_TILE == 0 and nb < 8
    nlat = s // ROW_TILE
    depth = len(layers)
    kinds = ("attn", "conformer", "shortconv")

    cond8 = jnp.zeros((8, d), F32).at[:nb].set(c).at[nb].set(c_ctx)
    tabs = [_mod_table(_adaln(cond8, p["ada_w"], p["ada_b"]), nb, d) for p in layers]
    rope = _rope_tables(s, l)

    h = _prologue(x, ctx, tabs[0], layers[0]["norm_g"])
    xs = (x, ctx)
    for i, p in enumerate(layers):
        kind = kinds[i % len(kinds)]
        last = i == depth - 1
        w_in = p["w_in"]
        w_out = p["w_out"].astype(BF16)
        h2 = h.reshape(nb * r, d)
        if kind == "attn":
            qkvz = _in_proj(h2, w_in, r, rope=rope, rope_cols=w_in.shape[1] // 2).reshape(nb, r, -1)
            u = _diff_attention(qkvz, p, i, s, l, keep_ctx=not last)
            if last:
                return _out_proj_final(u, w_out, xs, tabs[i], final_norm_g, s)
            xs, h = _out_proj(u, w_out, xs, tabs[i], tabs[i + 1], layers[i + 1]["norm_g"], nlat)
        else:
            assert not last and i > 0, "the stack starts and ends with an attention layer"
            proj = _in_proj(h2, w_in, r).reshape(nb, r, -1)
            layer = _conformer_layer if kind == "conformer" else _shortconv_layer
            xs, h = layer(proj, p, w_out, xs, tabs[i], tabs[i + 1], layers[i + 1]["norm_g"], s, l)


def kernel(x, c, ctx, c_ctx, l0_norm_g, l0_ada_w, l0_ada_b, l0_w_in, l0_lam_q1, l0_lam_k1, l0_lam_q2, l0_lam_k2, l0_head_g, l0_w_out, l1_norm_g, l1_ada_w, l1_ada_b, l1_w_in, l1_dw_w, l1_dw_b, l1_ln_g, l1_ln_b, l1_w_out, l2_norm_g, l2_ada_w, l2_ada_b, l2_w_in, l2_conv_w, l2_w_out, l3_norm_g, l3_ada_w, l3_ada_b, l3_w_in, l3_lam_q1, l3_lam_k1, l3_lam_q2, l3_lam_k2, l3_head_g, l3_w_out, final_norm_g):
    layers = [
        dict(norm_g=l0_norm_g, ada_w=l0_ada_w, ada_b=l0_ada_b, w_in=l0_w_in, lam_q1=l0_lam_q1, lam_k1=l0_lam_k1,
             lam_q2=l0_lam_q2, lam_k2=l0_lam_k2, head_g=l0_head_g, w_out=l0_w_out),
        dict(norm_g=l1_norm_g, ada_w=l1_ada_w, ada_b=l1_ada_b, w_in=l1_w_in, dw_w=l1_dw_w, dw_b=l1_dw_b,
             ln_g=l1_ln_g, ln_b=l1_ln_b, w_out=l1_w_out),
        dict(norm_g=l2_norm_g, ada_w=l2_ada_w, ada_b=l2_ada_b, w_in=l2_w_in, conv_w=l2_conv_w, w_out=l2_w_out),
        dict(norm_g=l3_norm_g, ada_w=l3_ada_w, ada_b=l3_ada_b, w_in=l3_w_in, lam_q1=l3_lam_q1, lam_k1=l3_lam_k1,
             lam_q2=l3_lam_q2, lam_k2=l3_lam_k2, head_g=l3_head_g, w_out=l3_w_out),
    ]
    return _forward(x, c, ctx, c_ctx, layers, final_norm_g)
```

```python
import functools
import math

import jax
import jax.numpy as jnp
from jax import lax
from jax.experimental import pallas as pl
from jax.experimental.pallas import tpu as pltpu

F32 = jnp.float32
BF16 = jnp.bfloat16

LANES = 128
HALO = 16
ROW_TILE = 256
DA_HEAD_DIM = 64
HEAD_W = 2 * DA_HEAD_DIM
ROPE_BASE = 10000.0
ROPE_FREQS = DA_HEAD_DIM // 4
GRID_W = 64
CF_KERNEL = 31
SC_KERNEL = 3
NORM_EPS = 1e-6
LN_EPS = 1e-5
Q_TILE = 2048
Q_SUB = 512
FINAL_TILE = 512
FIRST_OUT_TILE = 512
KV_CHUNK = 256
Q_SCALE = math.log2(math.e) / math.sqrt(DA_HEAD_DIM)
VMEM_LIMIT = 52 * 1024 * 1024


def _cparams(*sem):
    return pltpu.CompilerParams(dimension_semantics=sem, vmem_limit_bytes=VMEM_LIMIT)


def _silu(x):
    return x * jax.nn.sigmoid(x)


def _ada_kernel(c_ref, w_ref, b_ref, o_ref):
    a = _silu(c_ref[...]).astype(BF16)
    o_ref[...] = jnp.dot(a, w_ref[...].astype(BF16), preferred_element_type=F32) + b_ref[...]


def _adaln(cond8, w, b):
    d, n = w.shape
    tn = min(1024, n)
    return pl.pallas_call(
        _ada_kernel,
        grid=(n // tn,),
        in_specs=[pl.BlockSpec((8, d), lambda j: (0, 0)),
                  pl.BlockSpec((d, tn), lambda j: (0, j)),
                  pl.BlockSpec((1, tn), lambda j: (0, j))],
        out_specs=pl.BlockSpec((8, tn), lambda j: (0, j)),
        out_shape=jax.ShapeDtypeStruct((8, n), F32),
        compiler_params=_cparams("parallel"),
        name="adaln",
    )(cond8, w, b.reshape(1, n))


def _mod_table(m, nb, d):
    sh, sc, g = m[:, :d], m[:, d:2 * d], m[:, 2 * d:]
    rows = jnp.stack([1.0 + sc, sh, g], axis=1)
    lat = rows[:nb]
    ctx = jnp.broadcast_to(rows[nb][None], (nb, 3, d))
    tab = jnp.stack([lat, ctx], axis=1)
    return jnp.pad(tab, ((0, 0), (0, 0), (0, 5), (0, 0)))


def _norm_mod(x, g, scale1p, shift):
    ms = jnp.mean(x * x, axis=-1, keepdims=True)
    y = x * lax.rsqrt(ms + NORM_EPS)
    return (y * g) * scale1p + shift


def _pro_kernel(x_ref, c_ref, tl_ref, tc_ref, g_ref, h_ref, *, n_lat_tail):
    last = pl.num_programs(1) - 1
    g = g_ref[...]
    tl = tl_ref[0, 0]

    @pl.when(pl.program_id(1) < last)
    def _():
        h_ref[0] = _norm_mod(x_ref[0], g, tl[0:1], tl[1:2]).astype(BF16)

    @pl.when(pl.program_id(1) == last)
    def _():
        tc = tc_ref[0, 0]
        h_ref[0, 0:n_lat_tail, :] = _norm_mod(x_ref[0, 0:n_lat_tail, :], g, tl[0:1], tl[1:2]).astype(BF16)
        h_ref[0, n_lat_tail:, :] = _norm_mod(c_ref[0], g, tc[0:1], tc[1:2]).astype(BF16)


def _prologue(x, ctx, tab, norm_g):
    nb, s, d = x.shape
    l = ctx.shape[1]
    r = s + l
    tm = r // 4
    n_lat_tail = tm - l
    assert r % 4 == 0 and tm > l and n_lat_tail % HALO == 0 and s == 3 * tm + n_lat_tail
    tabspec = lambda kind: pl.BlockSpec((1, 1, 8, d), lambda b, t: (b, kind, 0, 0))
    return pl.pallas_call(
        functools.partial(_pro_kernel, n_lat_tail=n_lat_tail),
        grid=(nb, 4),
        in_specs=[pl.BlockSpec((1, tm, d), lambda b, t: (b, t, 0)),
                  pl.BlockSpec((1, l, d), lambda b, t: (b, 0, 0)),
                  tabspec(0), tabspec(1),
                  pl.BlockSpec((1, d), lambda b, t: (0, 0))],
        out_specs=pl.BlockSpec((1, tm, d), lambda b, t: (b, t, 0)),
        out_shape=jax.ShapeDtypeStruct((nb, r, d), BF16),
        compiler_params=_cparams("parallel", "arbitrary"),
        name="prologue",
    )(x, ctx, tab, tab, norm_g.reshape(1, d))


def _in_kernel(h_ref, w_ref, o_ref, wb_ref):
    @pl.when(pl.program_id(1) == 0)
    def _():
        wb_ref[...] = w_ref[...].astype(BF16)

    o_ref[...] = jnp.dot(h_ref[...], wb_ref[...], preferred_element_type=F32).astype(BF16)


def _in_rope_kernel(h_ref, w_ref, cos_ref, sa_ref, sb_ref, o_ref, wb_ref, *, n_rope, tn, sub):
    j = pl.program_id(0)
    tm = h_ref.shape[0]

    @pl.when(pl.program_id(1) == 0)
    def _():
        wb_ref[...] = w_ref[...].astype(BF16)

    @pl.when(j < n_rope)
    def _():
        for r0 in range(0, tm, sub):
            rows = slice(r0, r0 + sub)
            acc = jnp.dot(h_ref[rows, :], wb_ref[...], preferred_element_type=F32)
            cos, sa, sb = cos_ref[rows, :], sa_ref[rows, :], sb_ref[rows, :]
            for c in range(tn // LANES):
                xc = acc[:, c * LANES:(c + 1) * LANES]
                rot = (xc * cos + pltpu.roll(xc, LANES - ROPE_FREQS, 1) * sa
                       + pltpu.roll(xc, ROPE_FREQS, 1) * sb)
                o_ref[rows, c * LANES:(c + 1) * LANES] = rot.astype(BF16)

    @pl.when(j >= n_rope)
    def _():
        o_ref[...] = jnp.dot(h_ref[...], wb_ref[...], preferred_element_type=F32).astype(BF16)


def _in_tiles(t, r, d, n):
    tm = r // 4
    tn = 1024 if d >= 1024 else d
    assert tm % HALO == 0 and t % tm == 0 and n % tn == 0
    return tm, tn


def _in_proj(h, w, r, rope=None, rope_cols=0):
    t, d = h.shape
    n = w.shape[1]
    tm, tn = _in_tiles(t, r, d, n)
    grid = (n // tn, t // tm)
    h_spec = pl.BlockSpec((tm, d), lambda j, i: (i, 0))
    w_spec = pl.BlockSpec((d, tn), lambda j, i: (0, j))
    o_spec = pl.BlockSpec((tm, tn), lambda j, i: (i, j))
    out_shape = jax.ShapeDtypeStruct((t, n), BF16)
    scratch = [pltpu.VMEM((d, tn), BF16)]
    if rope is None:
        return pl.pallas_call(_in_kernel, grid=grid, in_specs=[h_spec, w_spec], out_specs=o_spec,
                              out_shape=out_shape, scratch_shapes=scratch,
                              compiler_params=_cparams("parallel", "arbitrary"), name="in_proj")(h, w)
    assert rope_cols % (2 * tn) == 0 and tn % LANES == 0
    per = r // tm
    n_q = rope_cols // (2 * tn)
    t_spec = pl.BlockSpec((tm, LANES), lambda j, i: (i % per + jnp.where(j >= n_q, per, 0), 0))
    sub = tm // 4 if (tm // 4) % HALO == 0 else tm
    kern = functools.partial(_in_rope_kernel, n_rope=rope_cols // tn, tn=tn, sub=sub)
    return pl.pallas_call(kern, grid=grid, in_specs=[h_spec, w_spec, t_spec, t_spec, t_spec],
                          out_specs=o_spec, out_shape=out_shape, scratch_shapes=scratch,
                          compiler_params=_cparams("parallel", "arbitrary"),
                          name="in_proj_rope")(h, w, *rope)


def _rope_tables(s, l):
    t = jnp.arange(s)
    row = (t // GRID_W).astype(F32)
    col = (t % GRID_W).astype(F32)
    inv = ROPE_BASE ** (-jnp.arange(ROPE_FREQS, dtype=F32) / ROPE_FREQS)
    lane = jnp.arange(LANES)
    dd = lane % DA_HEAD_DIM
    axis, half, f = dd // (2 * ROPE_FREQS), (dd % (2 * ROPE_FREQS)) // ROPE_FREQS, dd % ROPE_FREQS
    pos = jnp.where(axis[None, :] == 0, row[:, None], col[:, None])
    ang = pos * inv[f][None, :]
    cos, sin = jnp.cos(ang), jnp.sin(ang)
    sa = jnp.where(half[None, :] == 0, -sin, 0.0)
    sb = jnp.where(half[None, :] == 1, sin, 0.0)
    cos = jnp.concatenate([cos, jnp.ones((l, LANES), F32)], axis=0)
    sa = jnp.concatenate([sa, jnp.zeros((l, LANES), F32)], axis=0)
    sb = jnp.concatenate([sb, jnp.zeros((l, LANES), F32)], axis=0)
    return tuple(jnp.concatenate([t * Q_SCALE, t], axis=0) for t in (cos, sa, sb))


V_ROWS = HEAD_W + HALO


SAFE_DENOM = 2.0 ** -100
BOUND_SLACK = 1.01


def _split_comps(q):
    lane = lax.broadcasted_iota(jnp.int32, q.shape, 1)
    return jnp.concatenate([jnp.where((lane // DA_HEAD_DIM) == c, q, jnp.zeros_like(q)) for c in range(2)], axis=0)


def _nt_dot(a, b):
    return lax.dot_general(a, b, (((1,), (1,)), ((), ())), preferred_element_type=F32)


def _exact_chunks(k_ref, vt_ref, q01, chunks, ck):
    def scores(j):
        return _nt_dot(k_ref[0, j * ck:(j + 1) * ck, :], q01)

    s_next = scores(chunks[0])
    m = acc = None
    for i, j in enumerate(chunks):
        s = s_next
        if i + 1 < len(chunks):
            s_next = scores(chunks[i + 1])
        mc = jnp.max(s, axis=0, keepdims=True)
        m_new = mc if m is None else jnp.maximum(m, mc)
        p = jnp.exp2(s - m_new).astype(BF16)
        pv = jnp.dot(vt_ref[:, j * ck:(j + 1) * ck], p, preferred_element_type=F32)
        acc = pv if m is None else acc * jnp.exp2(m - m_new) + pv
        m = m_new
    return acc


def _bound_chunks(k_ref, vt_ref, q01, mb, chunks, ck):
    def scores(j):
        return _nt_dot(k_ref[0, j * ck:(j + 1) * ck, :], q01)

    acc = None
    s_next = scores(chunks[0])
    for i, j in enumerate(chunks):
        s = s_next
        if i + 1 < len(chunks):
            s_next = scores(chunks[i + 1])
        p = jnp.exp2(s - mb).astype(BF16)
        pv = jnp.dot(vt_ref[:, j * ck:(j + 1) * ck], p, preferred_element_type=F32)
        acc = pv if acc is None else acc + pv
    return acc


def _attn_finish(acc, lam, hg, z, lam_init):
    t = acc.shape[1] // 2
    o = acc[0:HEAD_W] * (1.0 / acc[HEAD_W:HEAD_W + 1])
    d = o[:, 0:t] - lam * o[:, t:2 * t]
    ms = jnp.mean(d * d, axis=0, keepdims=True)
    dn = (d * lax.rsqrt(ms + NORM_EPS)).T
    on = (dn * hg) * (1.0 - lam_init)
    return (on * _silu(z.astype(F32))).astype(BF16)


def _attn_body(lam_ref, hg_ref, q_ref, qc_ref, k_ref, v_ref, z_ref, zc_ref, o_ref, vt_ref, kn_ref, *,
               lam_init, ck, sub, keep_ctx):
    qi = pl.program_id(2)
    nkv = k_ref.shape[1]
    nchunk = nkv // ck

    @pl.when(qi == 0)
    def _():
        vt_ref[0:HEAD_W, :] = v_ref[0].astype(F32).T.astype(BF16)
        vt_ref[HEAD_W:V_ROWS, :] = jnp.ones((HALO, nkv), BF16)
        kf = k_ref[0].astype(F32)
        r_i = lax.broadcasted_iota(jnp.int32, (HEAD_W, HEAD_W), 0) // DA_HEAD_DIM
        c_i = lax.broadcasted_iota(jnp.int32, (HEAD_W, HEAD_W), 1) // DA_HEAD_DIM
        sel = jnp.where(r_i == c_i, 1.0, 0.0).astype(BF16)
        kn2 = jnp.dot((kf * kf).astype(BF16), sel, preferred_element_type=F32)
        kn_ref[...] = jnp.broadcast_to(jnp.sqrt(jnp.max(kn2, axis=0, keepdims=True)), kn_ref.shape)

    lp = lam_ref[...]
    lam = (jnp.exp(jnp.sum(lp[0:1] * lp[1:2], axis=-1, keepdims=True))
           - jnp.exp(jnp.sum(lp[2:3] * lp[3:4], axis=-1, keepdims=True)) + lam_init)
    hg = hg_ref[...]

    if keep_ctx:
        @pl.when(qi == 0)
        def _():
            acc = _exact_chunks(k_ref, vt_ref, _split_comps(qc_ref[0]), [nchunk - 1], ck)
            o_ref[0, 0:qc_ref.shape[1], :] = _attn_finish(acc, lam, hg, zc_ref[0], lam_init)

    @pl.when(qi >= (1 if keep_ctx else 0))
    def _():
        nsub = q_ref.shape[1] // sub
        kn = kn_ref[0:1, :]
        kmax = jnp.concatenate([jnp.broadcast_to(kn[:, 0:1], (1, sub)),
                                jnp.broadcast_to(kn[:, DA_HEAD_DIM:DA_HEAD_DIM + 1], (1, sub))], axis=1)
        denoms = []
        for t in range(nsub):
            rows = slice(t * sub, (t + 1) * sub)
            q01 = _split_comps(q_ref[0, rows, :])
            qf = q01.astype(F32)
            qn2 = _nt_dot(jnp.ones((HALO, HEAD_W), BF16), (qf * qf).astype(BF16))[0:1]
            mb = jnp.sqrt(qn2) * kmax * BOUND_SLACK
            acc = _bound_chunks(k_ref, vt_ref, q01, mb, range(nchunk), ck)
            o_ref[0, rows, :] = _attn_finish(acc, lam, hg, z_ref[0, rows, :], lam_init)
            denoms.append(jnp.min(acc[HEAD_W:HEAD_W + 1]))
        safe = functools.reduce(jnp.minimum, denoms) >= SAFE_DENOM

        @pl.when(jnp.logical_not(safe))
        def _():
            def redo(t, carry):
                rows = pl.ds(pl.multiple_of(t * sub, sub), sub)
                acc2 = _exact_chunks(k_ref, vt_ref, _split_comps(q_ref[0, rows, :]), range(nchunk), ck)
                o_ref[0, rows, :] = _attn_finish(acc2, lam, hg, z_ref[0, rows, :], lam_init)
                return carry

            lax.fori_loop(0, nsub, redo, 0)


def _diff_attention(qkvz, p, layer_idx, s, l, keep_ctx):
    nb, r, n4 = qkvz.shape
    nh = n4 // (4 * HEAD_W)
    lam_init = 0.8 - 0.6 * math.exp(-0.3 * layer_idx)
    lamp = jnp.stack([p["lam_q1"], p["lam_k1"], p["lam_q2"], p["lam_k2"]]).astype(F32)
    lamp = jnp.pad(lamp, ((0, 4), (0, LANES - DA_HEAD_DIM)))
    hg = p["head_g"].astype(F32).reshape(1, HEAD_W)
    tq = min(Q_TILE, s)
    ck = KV_CHUNK
    sub = min(Q_SUB, tq)
    assert s % tq == 0 and tq % sub == 0 and l == ck and r % ck == 0
    nlat_q = s // tq
    n_pre = 1 if keep_ctx else 0
    ctx_blk = s // l
    kern = functools.partial(_attn_body, lam_init=lam_init, ck=ck, sub=sub, keep_ctx=keep_ctx)
    const = lambda b, h, i: (0, 0)
    lat = lambda col: pl.BlockSpec((1, tq, HEAD_W),
                                   lambda b, h, i: (b, jnp.maximum(i - n_pre, 0), col * nh + h))
    ctx = lambda col: pl.BlockSpec((1, l, HEAD_W), lambda b, h, i: (b, ctx_blk, col * nh + h))
    kv = lambda col: pl.BlockSpec((1, r, HEAD_W), lambda b, h, i: (b, 0, col * nh + h))
    out_map = lambda b, h, i: (b, jnp.where(i < n_pre, nlat_q, i - n_pre), h)
    return pl.pallas_call(
        kern,
        grid=(nb, nh, nlat_q + n_pre),
        in_specs=[pl.BlockSpec((8, LANES), const), pl.BlockSpec((1, LANES), const),
                  lat(0), ctx(0), kv(1), kv(2), lat(3), ctx(3)],
        out_specs=pl.BlockSpec((1, tq, HEAD_W), out_map),
        out_shape=jax.ShapeDtypeStruct((nb, r if keep_ctx else s, nh * HEAD_W), BF16),
        scratch_shapes=[pltpu.VMEM((V_ROWS, r), BF16), pltpu.VMEM((8, LANES), F32)],
        compiler_params=_cparams("parallel", "parallel", "arbitrary"),
        name="diff_attn",
    )(lamp, hg, qkvz, qkvz, qkvz, qkvz, qkvz, qkvz)


MXU_K = 256


def _residual_out(u, w_ref, x, gate, tab_next, g_next, k_major=False):
    if k_major:
        y = None
        for k0 in range(0, u.shape[1], MXU_K):
            part = jnp.dot(u[:, k0:k0 + MXU_K], w_ref[k0:k0 + MXU_K, :], preferred_element_type=F32)
            y = part if y is None else y + part
    else:
        y = jnp.dot(u, w_ref[...], preferred_element_type=F32)
    xn = x + gate * y
    return xn, _norm_mod(xn, g_next, tab_next[0:1], tab_next[1:2]).astype(BF16)


def _out_first_kernel(ul_ref, uc_ref, w_ref, xl_ref, xc_ref, tgl_ref, tgc_ref, tnl_ref, tnc_ref, g_ref,
                      xo_ref, ho_ref):
    last = pl.num_programs(1) - 1
    g = g_ref[...]

    @pl.when(pl.program_id(1) < last)
    def _():
        xn, hn = _residual_out(ul_ref[0], w_ref, xl_ref[0], tgl_ref[0, 0][2:3], tnl_ref[0, 0], g)
        xo_ref[0] = xn
        ho_ref[0] = hn

    @pl.when(pl.program_id(1) == last)
    def _():
        n = uc_ref.shape[1]
        xn, hn = _residual_out(uc_ref[0], w_ref, xc_ref[0], tgc_ref[0, 0][2:3], tnc_ref[0, 0], g)
        xo_ref[0, 0:n, :] = xn
        ho_ref[0, 0:n, :] = hn


def _out_final_kernel(u_ref, w_ref, x_ref, tg_ref, g_ref, o_ref):
    y = jnp.dot(u_ref[0], w_ref[...], preferred_element_type=F32)
    xn = x_ref[0] + tg_ref[0, 0][2:3] * y
    ms = jnp.mean(xn * xn, axis=-1, keepdims=True)
    o_ref[0] = (xn * lax.rsqrt(ms + NORM_EPS)) * g_ref[...]


def _out_proj_first(u, w, x, ctx, tab, tab_next, norm_g_next):
    nb, r, dw = u.shape
    s, l = x.shape[1], ctx.shape[1]
    d = w.shape[1]
    tm = FIRST_OUT_TILE if s % FIRST_OUT_TILE == 0 else ROW_TILE
    assert s % tm == 0 and l <= tm and s % l == 0
    nl = s // tm
    lat = lambda width: pl.BlockSpec((1, tm, width), lambda b, t: (b, jnp.minimum(t, nl - 1), 0))
    tabspec = lambda kind: pl.BlockSpec((1, 1, 8, d), lambda b, t: (b, kind, 0, 0))
    row = lambda b, t: (b, t, 0)
    return pl.pallas_call(
        _out_first_kernel,
        grid=(nb, nl + 1),
        in_specs=[lat(dw), pl.BlockSpec((1, l, dw), lambda b, t: (b, s // l, 0)),
                  pl.BlockSpec((dw, d), lambda b, t: (0, 0)),
                  lat(d), pl.BlockSpec((1, l, d), lambda b, t: (b, 0, 0)),
                  tabspec(0), tabspec(1), tabspec(0), tabspec(1),
                  pl.BlockSpec((1, d), lambda b, t: (0, 0))],
        out_specs=[pl.BlockSpec((1, tm, d), row), pl.BlockSpec((1, tm, d), row)],
        out_shape=[jax.ShapeDtypeStruct((nb, r, d), F32), jax.ShapeDtypeStruct((nb, r, d), BF16)],
        compiler_params=_cparams("parallel", "arbitrary"),
        name="out_proj_first",
    )(u, u, w, x, ctx, tab, tab, tab_next, tab_next, norm_g_next.reshape(1, d))


def _out_proj_final(u, w, x, tab, final_g, s):
    nb, r, d = x.shape
    dw = u.shape[2]
    row = lambda b, t: (b, t, 0)
    tm = FINAL_TILE if s % FINAL_TILE == 0 else ROW_TILE
    return pl.pallas_call(
        _out_final_kernel,
        grid=(nb, s // tm),
        in_specs=[pl.BlockSpec((1, tm, dw), row),
                  pl.BlockSpec((dw, d), lambda b, t: (0, 0)),
                  pl.BlockSpec((1, tm, d), row),
                  pl.BlockSpec((1, 1, 8, d), lambda b, t: (b, 0, 0, 0)),
                  pl.BlockSpec((1, d), lambda b, t: (0, 0))],
        out_specs=pl.BlockSpec((1, tm, d), row),
        out_shape=jax.ShapeDtypeStruct((nb, s, d), F32),
        compiler_params=_cparams("parallel", "parallel"),
        name="out_proj_final",
    )(u, w, x, tab, final_g.reshape(1, d))


def _halo_valid(nlat):
    t = pl.program_id(1)
    prev_ok = jnp.logical_and(t != 0, t != nlat)
    next_ok = jnp.logical_and(t != nlat - 1, t != nlat)
    return prev_ok.astype(F32), next_ok.astype(F32)


def _conf_out_kernel(pa_ref, ca_ref, na_ref, pb_ref, cb_ref, nb_ref, z_ref, w_ref, wb_ref, lg_ref, lb_ref,
                     wo_ref, x_ref, tg_ref, tn_ref, g_ref, xo_ref, ho_ref, win_ref, u_ref, *, nlat):
    tm = ca_ref.shape[1]
    nc = win_ref.shape[0]
    pv, nv = _halo_valid(nlat)

    def glu(a_ref, b_ref):
        return a_ref[0].astype(F32) * jax.nn.sigmoid(b_ref[0].astype(F32))

    gp, gc, gn = glu(pa_ref, pb_ref) * pv, glu(ca_ref, cb_ref), glu(na_ref, nb_ref) * nv
    for c in range(nc):
        sl = slice(c * LANES, (c + 1) * LANES)
        win_ref[c, 0:HALO, :] = gp[:, sl]
        win_ref[c, HALO:HALO + tm, :] = gc[:, sl]
        win_ref[c, HALO + tm:2 * HALO + tm, :] = gn[:, sl]

    off = HALO - CF_KERNEL // 2

    def conv_chunk(c, carry):
        w = w_ref[c]
        acc = jnp.zeros((tm, LANES), F32)
        for k in range(CF_KERNEL):
            acc = acc + win_ref[c, pl.ds(off + k, tm), :] * w[k:k + 1, :]
        u_ref[c] = acc + wb_ref[c][0:1]
        return carry

    lax.fori_loop(0, nc, conv_chunk, 0)

    d = nc * LANES
    s1 = u_ref[0]
    for c in range(1, nc):
        s1 = s1 + u_ref[c]
    mu = jnp.sum(s1, axis=-1, keepdims=True) / d
    s2 = jnp.square(u_ref[0] - mu)
    for c in range(1, nc):
        s2 = s2 + jnp.square(u_ref[c] - mu)
    rs = lax.rsqrt(jnp.sum(s2, axis=-1, keepdims=True) / d + LN_EPS)
    chunks = []
    for c in range(nc):
        sl = slice(c * LANES, (c + 1) * LANES)
        y = ((u_ref[c] - mu) * rs) * lg_ref[:, sl] + lb_ref[:, sl]
        zc = z_ref[0, :, sl].astype(F32)
        chunks.append((_silu(y) * _silu(zc)).astype(BF16))
    xo_ref[0], ho_ref[0] = _residual_out(jnp.concatenate(chunks, axis=1), wo_ref, x_ref[0], tg_ref[0, 0][2:3],
                                         tn_ref[0, 0], g_ref[...], k_major=True)


def _halo_specs(tm, d, col, r):
    nblk = tm // HALO
    last_blk = r // HALO - 1
    prev = pl.BlockSpec((1, HALO, d), lambda b, t: (b, jnp.maximum(t * nblk - 1, 0), col))
    cur = pl.BlockSpec((1, tm, d), lambda b, t: (b, t, col))
    nxt = pl.BlockSpec((1, HALO, d), lambda b, t: (b, jnp.minimum((t + 1) * nblk, last_blk), col))
    return [prev, cur, nxt]


def _out_specs_and_args(w_out, x, tab, tab_next, norm_g_next, nlat):
    nb, r, d = x.shape
    row = lambda b, t: (b, t, 0)
    tabspec = pl.BlockSpec((1, 1, 8, d), lambda b, t: (b, t // nlat, 0, 0))
    in_specs = [pl.BlockSpec(w_out.shape, lambda b, t: (0, 0)), pl.BlockSpec((1, ROW_TILE, d), row),
                tabspec, tabspec, pl.BlockSpec((1, d), lambda b, t: (0, 0))]
    args = [w_out, x, tab, tab_next, norm_g_next.reshape(1, d)]
    out_specs = [pl.BlockSpec((1, ROW_TILE, d), row), pl.BlockSpec((1, ROW_TILE, d), row)]
    out_shape = [jax.ShapeDtypeStruct((nb, r, d), F32), jax.ShapeDtypeStruct((nb, r, d), BF16)]
    return in_specs, args, out_specs, out_shape


def _conformer_layer(abz, p, w_out, x, tab, tab_next, norm_g_next, s, l):
    nb, r, d3 = abz.shape
    d = d3 // 3
    tm = ROW_TILE
    assert l == tm and s % tm == 0 and d % LANES == 0
    nc = d // LANES
    nlat = s // tm
    specs = _halo_specs(tm, d, 0, r) + _halo_specs(tm, d, 1, r)
    z_spec = pl.BlockSpec((1, tm, d), lambda b, t: (b, t, 2))
    w = jnp.pad(p["dw_w"].astype(F32), ((0, 1), (0, 0))).reshape(CF_KERNEL + 1, nc, LANES).transpose(1, 0, 2)
    wb = jnp.broadcast_to(p["dw_b"].astype(F32).reshape(nc, 1, LANES), (nc, 8, LANES))
    full = lambda shape: pl.BlockSpec(shape, lambda b, t: (0,) * len(shape))
    o_in, o_args, out_specs, out_shape = _out_specs_and_args(w_out, x, tab, tab_next, norm_g_next, nlat)
    return pl.pallas_call(
        functools.partial(_conf_out_kernel, nlat=nlat),
        grid=(nb, r // tm),
        in_specs=specs + [z_spec, full((nc, CF_KERNEL + 1, LANES)), full((nc, 8, LANES)),
                          full((1, d)), full((1, d))] + o_in,
        out_specs=out_specs,
        out_shape=out_shape,
        scratch_shapes=[pltpu.VMEM((nc, tm + 2 * HALO, LANES), F32), pltpu.VMEM((nc, tm, LANES), F32)],
        compiler_params=_cparams("parallel", "parallel"),
        name="conformer_layer",
    )(abz, abz, abz, abz, abz, abz, abz, w, wb, p["ln_g"].astype(F32).reshape(1, d),
      p["ln_b"].astype(F32).reshape(1, d), *o_args)


def _sc_out_kernel(bg_ref, pc_ref, cc_ref, nc_ref, pv_ref, cv_ref, nv_ref, z_ref, w_ref,
                   wo_ref, x_ref, tg_ref, tn_ref, g_ref, xo_ref, ho_ref, win_ref, *, nlat):
    tm = cc_ref.shape[1]
    nc = win_ref.shape[0]
    pvalid, nvalid = _halo_valid(nlat)

    def prod(c_ref, v_ref):
        return c_ref[0].astype(F32) * v_ref[0].astype(F32)

    wp, wc, wn = prod(pc_ref, pv_ref) * pvalid, prod(cc_ref, cv_ref), prod(nc_ref, nv_ref) * nvalid
    off = HALO - SC_KERNEL // 2
    w = w_ref[...]
    chunks = []
    for c in range(nc):
        sl = slice(c * LANES, (c + 1) * LANES)
        win_ref[c, 0:HALO, :] = wp[:, sl]
        win_ref[c, HALO:HALO + tm, :] = wc[:, sl]
        win_ref[c, HALO + tm:2 * HALO + tm, :] = wn[:, sl]
        y = win_ref[c, pl.ds(off, tm), :] * w[0:1, sl]
        for k in range(1, SC_KERNEL):
            y = y + win_ref[c, pl.ds(off + k, tm), :] * w[k:k + 1, sl]
        chunks.append(((bg_ref[0, :, sl].astype(F32) * y) * _silu(z_ref[0, :, sl].astype(F32))).astype(BF16))
    u = jnp.concatenate(chunks, axis=1)
    xo_ref[0], ho_ref[0] = _residual_out(u, wo_ref, x_ref[0], tg_ref[0, 0][2:3], tn_ref[0, 0], g_ref[...],
                                         k_major=True)


def _shortconv_layer(bcvz, p, w_out, x, tab, tab_next, norm_g_next, s, l):
    nb, r, d4 = bcvz.shape
    d = d4 // 4
    tm = ROW_TILE
    assert l == tm and s % tm == 0
    nlat = s // tm
    specs = [pl.BlockSpec((1, tm, d), lambda b, t: (b, t, 0))]
    specs += _halo_specs(tm, d, 1, r) + _halo_specs(tm, d, 2, r)
    specs.append(pl.BlockSpec((1, tm, d), lambda b, t: (b, t, 3)))
    specs.append(pl.BlockSpec((8, d), lambda b, t: (0, 0)))
    w = jnp.pad(p["conv_w"].astype(F32), ((0, 8 - SC_KERNEL), (0, 0)))
    o_in, o_args, out_specs, out_shape = _out_specs_and_args(w_out, x, tab, tab_next, norm_g_next, nlat)
    return pl.pallas_call(
        functools.partial(_sc_out_kernel, nlat=nlat),
        grid=(nb, r // tm),
        in_specs=specs + o_in,
        out_specs=out_specs,
        out_shape=out_shape,
        scratch_shapes=[pltpu.VMEM((d // LANES, tm + 2 * HALO, LANES), F32)],
        compiler_params=_cparams("parallel", "parallel"),
        name="shortconv_layer",
    )(bcvz, bcvz, bcvz, bcvz, bcvz, bcvz, bcvz, bcvz, w, *o_args)


def _forward(x, c, ctx, c_ctx, layers, final_norm_g):
    nb, s, d = x.shape
    l = ctx.shape[1]
    r = s + l
    assert l == ROW_TILE and s % ROW_TILE == 0 and nb < 8
    depth = len(layers)
    kinds = ("attn", "conformer", "shortconv")

    cond8 = jnp.zeros((8, d), F32).at[:nb].set(c).at[nb].set(c_ctx)
    tabs = [_mod_table(_adaln(cond8, p["ada_w"], p["ada_b"]), nb, d) for p in layers]
    rope = _rope_tables(s, l)

    h = _prologue(x, ctx, tabs[0], layers[0]["norm_g"])
    xs = None
    for i, p in enumerate(layers):
        kind = kinds[i % len(kinds)]
        last = i == depth - 1
        w_in = p["w_in"]
        w_out = p["w_out"].astype(BF16)
        h2 = h.reshape(nb * r, d)
        if kind == "attn":
            qkvz = _in_proj(h2, w_in, r, rope=rope, rope_cols=w_in.shape[1] // 2).reshape(nb, r, -1)
            u = _diff_attention(qkvz, p, i, s, l, keep_ctx=not last)
            if last:
                return _out_proj_final(u, w_out, xs, tabs[i], final_norm_g, s)
            assert i == 0, "attention layers sit at the two ends of the stack"
            xs, h = _out_proj_first(u, w_out, x, ctx, tabs[i], tabs[i + 1], layers[i + 1]["norm_g"])
        else:
            assert not last and i > 0, "the stack starts and ends with an attention layer"
            proj = _in_proj(h2, w_in, r).reshape(nb, r, -1)
            layer = _conformer_layer if kind == "conformer" else _shortconv_layer
            xs, h = layer(proj, p, w_out, xs, tabs[i], tabs[i + 1], layers[i + 1]["norm_g"], s, l)


def kernel(x, c, ctx, c_ctx, l0_norm_g, l0_ada_w, l0_ada_b, l0_w_in, l0_lam_q1, l0_lam_k1, l0_lam_q2, l0_lam_k2, l0_head_g, l0_w_out, l1_norm_g, l1_ada_w, l1_ada_b, l1_w_in, l1_dw_w, l1_dw_b, l1_ln_g, l1_ln_b, l1_w_out, l2_norm_g, l2_ada_w, l2_ada_b, l2_w_in, l2_conv_w, l2_w_out, l3_norm_g, l3_ada_w, l3_ada_b, l3_w_in, l3_lam_q1, l3_lam_k1, l3_lam_q2, l3_lam_k2, l3_head_g, l3_w_out, final_norm_g):
    layers = [
        dict(norm_g=l0_norm_g, ada_w=l0_ada_w, ada_b=l0_ada_b, w_in=l0_w_in, lam_q1=l0_lam_q1, lam_k1=l0_lam_k1,
             lam_q2=l0_lam_q2, lam_k2=l0_lam_k2, head_g=l0_head_g, w_out=l0_w_out),
        dict(norm_g=l1_norm_g, ada_w=l1_ada_w, ada_b=l1_ada_b, w_in=l1_w_in, dw_w=l1_dw_w, dw_b=l1_dw_b,
             ln_g=l1_ln_g, ln_b=l1_ln_b, w_out=l1_w_out),
        dict(norm_g=l2_norm_g, ada_w=l2_ada_w, ada_b=l2_ada_b, w_in=l2_w_in, conv_w=l2_conv_w, w_out=l2_w_out),
        dict(norm_g=l3_norm_g, ada_w=l3_ada_w, ada_b=l3_ada_b, w_in=l3_w_in, lam_q1=l3_lam_q1, lam_k1=l3_lam_k1,
             lam_q2=l3_lam_q2, lam_k2=l3_lam_k2, head_g=l3_head_g, w_out=l3_w_out),
    ]
    return _forward(x, c, ctx, c_ctx, layers, final_norm_g)
```

```python
import functools
import math

import jax
import jax.numpy as jnp
from jax import lax
from jax.experimental import pallas as pl
from jax.experimental.pallas import tpu as pltpu

F32 = jnp.float32
BF16 = jnp.bfloat16

LANES = 128
HALO = 16
ROW_TILE = 256
DA_HEAD_DIM = 64
HEAD_W = 2 * DA_HEAD_DIM
ROPE_BASE = 10000.0
ROPE_FREQS = DA_HEAD_DIM // 4
GRID_W = 64
CF_KERNEL = 31
SC_KERNEL = 3
NORM_EPS = 1e-6
LN_EPS = 1e-5
Q_TILE = 2048
Q_SUB = 512
FINAL_TILE = 512
FIRST_OUT_TILE = 512
KV_CHUNK = 256
Q_SCALE = math.log2(math.e) / math.sqrt(DA_HEAD_DIM)
VMEM_LIMIT = 52 * 1024 * 1024


def _cparams(*sem):
    return pltpu.CompilerParams(dimension_semantics=sem, vmem_limit_bytes=VMEM_LIMIT)


def _silu(x):
    return x * jax.nn.sigmoid(x)


def _ada_kernel(c_ref, w_ref, b_ref, o_ref):
    a = _silu(c_ref[...]).astype(BF16)
    o_ref[...] = jnp.dot(a, w_ref[...].astype(BF16), preferred_element_type=F32) + b_ref[...]


def _adaln(cond8, w, b):
    d, n = w.shape
    tn = min(1024, n)
    return pl.pallas_call(
        _ada_kernel,
        grid=(n // tn,),
        in_specs=[pl.BlockSpec((8, d), lambda j: (0, 0)),
                  pl.BlockSpec((d, tn), lambda j: (0, j)),
                  pl.BlockSpec((1, tn), lambda j: (0, j))],
        out_specs=pl.BlockSpec((8, tn), lambda j: (0, j)),
        out_shape=jax.ShapeDtypeStruct((8, n), F32),
        compiler_params=_cparams("parallel"),
        name="adaln",
    )(cond8, w, b.reshape(1, n))


def _mod_table(m, nb, d):
    sh, sc, g = m[:, :d], m[:, d:2 * d], m[:, 2 * d:]
    rows = jnp.stack([1.0 + sc, sh, g], axis=1)
    lat = rows[:nb]
    ctx = jnp.broadcast_to(rows[nb][None], (nb, 3, d))
    tab = jnp.stack([lat, ctx], axis=1)
    return jnp.pad(tab, ((0, 0), (0, 0), (0, 5), (0, 0)))


def _norm_mod(x, g, scale1p, shift):
    ms = jnp.mean(x * x, axis=-1, keepdims=True)
    y = x * lax.rsqrt(ms + NORM_EPS)
    return (y * g) * scale1p + shift


def _pro_kernel(x_ref, c_ref, tl_ref, tc_ref, g_ref, h_ref, *, n_lat_tail):
    last = pl.num_programs(1) - 1
    g = g_ref[...]
    tl = tl_ref[0, 0]

    @pl.when(pl.program_id(1) < last)
    def _():
        h_ref[0] = _norm_mod(x_ref[0], g, tl[0:1], tl[1:2]).astype(BF16)

    @pl.when(pl.program_id(1) == last)
    def _():
        tc = tc_ref[0, 0]
        h_ref[0, 0:n_lat_tail, :] = _norm_mod(x_ref[0, 0:n_lat_tail, :], g, tl[0:1], tl[1:2]).astype(BF16)
        h_ref[0, n_lat_tail:, :] = _norm_mod(c_ref[0], g, tc[0:1], tc[1:2]).astype(BF16)


def _prologue(x, ctx, tab, norm_g):
    nb, s, d = x.shape
    l = ctx.shape[1]
    r = s + l
    tm = r // 4
    n_lat_tail = tm - l
    assert r % 4 == 0 and tm > l and n_lat_tail % HALO == 0 and s == 3 * tm + n_lat_tail
    tabspec = lambda kind: pl.BlockSpec((1, 1, 8, d), lambda b, t: (b, kind, 0, 0))
    return pl.pallas_call(
        functools.partial(_pro_kernel, n_lat_tail=n_lat_tail),
        grid=(nb, 4),
        in_specs=[pl.BlockSpec((1, tm, d), lambda b, t: (b, t, 0)),
                  pl.BlockSpec((1, l, d), lambda b, t: (b, 0, 0)),
                  tabspec(0), tabspec(1),
                  pl.BlockSpec((1, d), lambda b, t: (0, 0))],
        out_specs=pl.BlockSpec((1, tm, d), lambda b, t: (b, t, 0)),
        out_shape=jax.ShapeDtypeStruct((nb, r, d), BF16),
        compiler_params=_cparams("parallel", "arbitrary"),
        name="prologue",
    )(x, ctx, tab, tab, norm_g.reshape(1, d))


def _in_kernel(h_ref, w_ref, o_ref, wb_ref):
    @pl.when(pl.program_id(1) == 0)
    def _():
        wb_ref[...] = w_ref[...].astype(BF16)

    o_ref[...] = jnp.dot(h_ref[...], wb_ref[...], preferred_element_type=F32).astype(BF16)


def _in_rope_kernel(h_ref, w_ref, cos_ref, sa_ref, sb_ref, o_ref, wb_ref, *, n_rope, tn, sub):
    j = pl.program_id(0)
    tm = h_ref.shape[0]

    @pl.when(pl.program_id(1) == 0)
    def _():
        wb_ref[...] = w_ref[...].astype(BF16)

    @pl.when(j < n_rope)
    def _():
        for r0 in range(0, tm, sub):
            rows = slice(r0, r0 + sub)
            acc = jnp.dot(h_ref[rows, :], wb_ref[...], preferred_element_type=F32)
            cos, sa, sb = cos_ref[rows, :], sa_ref[rows, :], sb_ref[rows, :]
            for c in range(tn // LANES):
                xc = acc[:, c * LANES:(c + 1) * LANES]
                rot = (xc * cos + pltpu.roll(xc, LANES - ROPE_FREQS, 1) * sa
                       + pltpu.roll(xc, ROPE_FREQS, 1) * sb)
                o_ref[rows, c * LANES:(c + 1) * LANES] = rot.astype(BF16)

    @pl.when(j >= n_rope)
    def _():
        o_ref[...] = jnp.dot(h_ref[...], wb_ref[...], preferred_element_type=F32).astype(BF16)


def _in_tiles(t, r, d, n):
    tm = r // 4
    tn = 1024 if d >= 1024 else d
    assert tm % HALO == 0 and t % tm == 0 and n % tn == 0
    return tm, tn


def _in_proj(h, w, r, rope=None, rope_cols=0):
    t, d = h.shape
    n = w.shape[1]
    tm, tn = _in_tiles(t, r, d, n)
    grid = (n // tn, t // tm)
    h_spec = pl.BlockSpec((tm, d), lambda j, i: (i, 0))
    w_spec = pl.BlockSpec((d, tn), lambda j, i: (0, j))
    o_spec = pl.BlockSpec((tm, tn), lambda j, i: (i, j))
    out_shape = jax.ShapeDtypeStruct((t, n), BF16)
    scratch = [pltpu.VMEM((d, tn), BF16)]
    if rope is None:
        return pl.pallas_call(_in_kernel, grid=grid, in_specs=[h_spec, w_spec], out_specs=o_spec,
                              out_shape=out_shape, scratch_shapes=scratch,
                              compiler_params=_cparams("parallel", "arbitrary"), name="in_proj")(h, w)
    assert rope_cols % (2 * tn) == 0 and tn % LANES == 0
    per = r // tm
    n_q = rope_cols // (2 * tn)
    n_rope = rope_cols // tn
    t_spec = pl.BlockSpec(
        (tm, LANES), lambda j, i: (jnp.where(j < n_rope, i % per + jnp.where(j >= n_q, per, 0), 2 * per - 1), 0))
    sub = tm // 4 if (tm // 4) % HALO == 0 else tm
    kern = functools.partial(_in_rope_kernel, n_rope=n_rope, tn=tn, sub=sub)
    return pl.pallas_call(kern, grid=grid, in_specs=[h_spec, w_spec, t_spec, t_spec, t_spec],
                          out_specs=o_spec, out_shape=out_shape, scratch_shapes=scratch,
                          compiler_params=_cparams("parallel", "arbitrary"),
                          name="in_proj_rope")(h, w, *rope)


def _rope_tables(s, l):
    t = jnp.arange(s)
    row = (t // GRID_W).astype(F32)
    col = (t % GRID_W).astype(F32)
    inv = ROPE_BASE ** (-jnp.arange(ROPE_FREQS, dtype=F32) / ROPE_FREQS)
    lane = jnp.arange(LANES)
    dd = lane % DA_HEAD_DIM
    axis, half, f = dd // (2 * ROPE_FREQS), (dd % (2 * ROPE_FREQS)) // ROPE_FREQS, dd % ROPE_FREQS
    pos = jnp.where(axis[None, :] == 0, row[:, None], col[:, None])
    ang = pos * inv[f][None, :]
    cos, sin = jnp.cos(ang), jnp.sin(ang)
    sa = jnp.where(half[None, :] == 0, -sin, 0.0)
    sb = jnp.where(half[None, :] == 1, sin, 0.0)
    cos = jnp.concatenate([cos, jnp.ones((l, LANES), F32)], axis=0)
    sa = jnp.concatenate([sa, jnp.zeros((l, LANES), F32)], axis=0)
    sb = jnp.concatenate([sb, jnp.zeros((l, LANES), F32)], axis=0)
    return tuple(jnp.concatenate([t * Q_SCALE, t], axis=0) for t in (cos, sa, sb))


V_ROWS = HEAD_W + HALO
SAFE_DENOM = 2.0 ** -100
BOUND_SLACK = 1.01


def _split_comps(q):
    lane = lax.broadcasted_iota(jnp.int32, q.shape, 1)
    return jnp.concatenate([jnp.where((lane // DA_HEAD_DIM) == c, q, jnp.zeros_like(q)) for c in range(2)], axis=0)


def _nt_dot(a, b):
    return lax.dot_general(a, b, (((1,), (1,)), ((), ())), preferred_element_type=F32)


def _exact_chunks(k_ref, vt_ref, q01, chunks, ck):
    def scores(j):
        return _nt_dot(k_ref[0, j * ck:(j + 1) * ck, :], q01)

    s_next = scores(chunks[0])
    m = acc = None
    for i, j in enumerate(chunks):
        s = s_next
        if i + 1 < len(chunks):
            s_next = scores(chunks[i + 1])
        mc = jnp.max(s, axis=0, keepdims=True)
        m_new = mc if m is None else jnp.maximum(m, mc)
        p = jnp.exp2(s - m_new).astype(BF16)
        pv = jnp.dot(vt_ref[:, j * ck:(j + 1) * ck], p, preferred_element_type=F32)
        acc = pv if m is None else acc * jnp.exp2(m - m_new) + pv
        m = m_new
    return acc


def _bound_chunks(k_ref, vt_ref, q01, mb, chunks, ck):
    def scores(j):
        return _nt_dot(k_ref[0, j * ck:(j + 1) * ck, :], q01)

    acc = None
    s_next = scores(chunks[0])
    for i, j in enumerate(chunks):
        s = s_next
        if i + 1 < len(chunks):
            s_next = scores(chunks[i + 1])
        p = jnp.exp2(s - mb).astype(BF16)
        pv = jnp.dot(vt_ref[:, j * ck:(j + 1) * ck], p, preferred_element_type=F32)
        acc = pv if acc is None else acc + pv
    return acc


def _attn_finish(acc, lam, hg, z, lam_init):
    t = acc.shape[1] // 2
    o = acc[0:HEAD_W] * (1.0 / acc[HEAD_W:HEAD_W + 1])
    d = o[:, 0:t] - lam * o[:, t:2 * t]
    ms = jnp.mean(d * d, axis=0, keepdims=True)
    dn = (d * lax.rsqrt(ms + NORM_EPS)).T
    on = (dn * hg) * (1.0 - lam_init)
    return (on * _silu(z.astype(F32))).astype(BF16)


def _attn_body(lam_ref, hg_ref, q_ref, qc_ref, k_ref, v_ref, z_ref, zc_ref, o_ref, vt_ref, kn_ref, *,
               lam_init, ck, sub, keep_ctx):
    qi = pl.program_id(2)
    nkv = k_ref.shape[1]
    nchunk = nkv // ck

    @pl.when(qi == 0)
    def _():
        vt_ref[0:HEAD_W, :] = v_ref[0].astype(F32).T.astype(BF16)
        vt_ref[HEAD_W:V_ROWS, :] = jnp.ones((HALO, nkv), BF16)
        kf = k_ref[0].astype(F32)
        r_i = lax.broadcasted_iota(jnp.int32, (HEAD_W, HEAD_W), 0) // DA_HEAD_DIM
        c_i = lax.broadcasted_iota(jnp.int32, (HEAD_W, HEAD_W), 1) // DA_HEAD_DIM
        sel = jnp.where(r_i == c_i, 1.0, 0.0).astype(BF16)
        kn2 = jnp.dot((kf * kf).astype(BF16), sel, preferred_element_type=F32)
        kn_ref[...] = jnp.broadcast_to(jnp.sqrt(jnp.max(kn2, axis=0, keepdims=True)), kn_ref.shape)

    lp = lam_ref[...]
    lam = (jnp.exp(jnp.sum(lp[0:1] * lp[1:2], axis=-1, keepdims=True))
           - jnp.exp(jnp.sum(lp[2:3] * lp[3:4], axis=-1, keepdims=True)) + lam_init)
    hg = hg_ref[...]

    if keep_ctx:
        @pl.when(qi == 0)
        def _():
            acc = _exact_chunks(k_ref, vt_ref, _split_comps(qc_ref[0]), [nchunk - 1], ck)
            o_ref[0, 0:qc_ref.shape[1], :] = _attn_finish(acc, lam, hg, zc_ref[0], lam_init)

    @pl.when(qi >= (1 if keep_ctx else 0))
    def _():
        nsub = q_ref.shape[1] // sub
        kn = kn_ref[0:1, :]
        kmax = jnp.concatenate([jnp.broadcast_to(kn[:, 0:1], (1, sub)),
                                jnp.broadcast_to(kn[:, DA_HEAD_DIM:DA_HEAD_DIM + 1], (1, sub))], axis=1)
        denoms = []
        for t in range(nsub):
            rows = slice(t * sub, (t + 1) * sub)
            q01 = _split_comps(q_ref[0, rows, :])
            qf = q01.astype(F32)
            qn2 = _nt_dot(jnp.ones((HALO, HEAD_W), BF16), (qf * qf).astype(BF16))[0:1]
            mb = jnp.sqrt(qn2) * kmax * BOUND_SLACK
            acc = _bound_chunks(k_ref, vt_ref, q01, mb, range(nchunk), ck)
            o_ref[0, rows, :] = _attn_finish(acc, lam, hg, z_ref[0, rows, :], lam_init)
            denoms.append(jnp.min(acc[HEAD_W:HEAD_W + 1]))
        safe = functools.reduce(jnp.minimum, denoms) >= SAFE_DENOM

        @pl.when(jnp.logical_not(safe))
        def _():
            def redo(t, carry):
                rows = pl.ds(pl.multiple_of(t * sub, sub), sub)
                acc2 = _exact_chunks(k_ref, vt_ref, _split_comps(q_ref[0, rows, :]), range(nchunk), ck)
                o_ref[0, rows, :] = _attn_finish(acc2, lam, hg, z_ref[0, rows, :], lam_init)
                return carry

            lax.fori_loop(0, nsub, redo, 0)


def _diff_attention(qkvz, p, layer_idx, s, l, keep_ctx):
    nb, r, n4 = qkvz.shape
    nh = n4 // (4 * HEAD_W)
    lam_init = 0.8 - 0.6 * math.exp(-0.3 * layer_idx)
    lamp = jnp.stack([p["lam_q1"], p["lam_k1"], p["lam_q2"], p["lam_k2"]]).astype(F32)
    lamp = jnp.pad(lamp, ((0, 4), (0, LANES - DA_HEAD_DIM)))
    hg = p["head_g"].astype(F32).reshape(1, HEAD_W)
    tq = min(Q_TILE, s)
    ck = KV_CHUNK
    sub = min(Q_SUB, tq)
    assert s % tq == 0 and tq % sub == 0 and l == ck and r % ck == 0
    nlat_q = s // tq
    n_pre = 1 if keep_ctx else 0
    ctx_blk = s // l
    kern = functools.partial(_attn_body, lam_init=lam_init, ck=ck, sub=sub, keep_ctx=keep_ctx)
    const = lambda b, h, i: (0, 0)
    lat = lambda col: pl.BlockSpec((1, tq, HEAD_W),
                                   lambda b, h, i: (b, jnp.maximum(i - n_pre, 0), col * nh + h))
    ctx = lambda col: pl.BlockSpec((1, l, HEAD_W), lambda b, h, i: (b, ctx_blk, col * nh + h))
    kv = lambda col: pl.BlockSpec((1, r, HEAD_W), lambda b, h, i: (b, 0, col * nh + h))
    out_map = lambda b, h, i: (b, jnp.where(i < n_pre, nlat_q, i - n_pre), h)
    return pl.pallas_call(
        kern,
        grid=(nb, nh, nlat_q + n_pre),
        in_specs=[pl.BlockSpec((8, LANES), const), pl.BlockSpec((1, LANES), const),
                  lat(0), ctx(0), kv(1), kv(2), lat(3), ctx(3)],
        out_specs=pl.BlockSpec((1, tq, HEAD_W), out_map),
        out_shape=jax.ShapeDtypeStruct((nb, r if keep_ctx else s, nh * HEAD_W), BF16),
        scratch_shapes=[pltpu.VMEM((V_ROWS, r), BF16), pltpu.VMEM((8, LANES), F32)],
        compiler_params=_cparams("parallel", "parallel", "arbitrary"),
        name="diff_attn",
    )(lamp, hg, qkvz, qkvz, qkvz, qkvz, qkvz, qkvz)


MXU_K = 256


def _residual_out(u, w_ref, x, gate, tab_next, g_next, k_major=False):
    if k_major:
        y = None
        for k0 in range(0, u.shape[1], MXU_K):
            part = jnp.dot(u[:, k0:k0 + MXU_K], w_ref[k0:k0 + MXU_K, :], preferred_element_type=F32)
            y = part if y is None else y + part
    else:
        y = jnp.dot(u, w_ref[...], preferred_element_type=F32)
    xn = x + gate * y
    return xn, _norm_mod(xn, g_next, tab_next[0:1], tab_next[1:2]).astype(BF16)


def _out_first_kernel(ul_ref, uc_ref, w_ref, xl_ref, xc_ref, tgl_ref, tgc_ref, tnl_ref, tnc_ref, g_ref,
                      xo_ref, ho_ref):
    last = pl.num_programs(1) - 1
    g = g_ref[...]

    @pl.when(pl.program_id(1) < last)
    def _():
        xn, hn = _residual_out(ul_ref[0], w_ref, xl_ref[0], tgl_ref[0, 0][2:3], tnl_ref[0, 0], g)
        xo_ref[0] = xn
        ho_ref[0] = hn

    @pl.when(pl.program_id(1) == last)
    def _():
        n = uc_ref.shape[1]
        xn, hn = _residual_out(uc_ref[0], w_ref, xc_ref[0], tgc_ref[0, 0][2:3], tnc_ref[0, 0], g)
        xo_ref[0, 0:n, :] = xn
        ho_ref[0, 0:n, :] = hn


def _out_final_kernel(u_ref, w_ref, x_ref, tg_ref, g_ref, o_ref):
    y = jnp.dot(u_ref[0], w_ref[...], preferred_element_type=F32)
    xn = x_ref[0] + tg_ref[0, 0][2:3] * y
    ms = jnp.mean(xn * xn, axis=-1, keepdims=True)
    o_ref[0] = (xn * lax.rsqrt(ms + NORM_EPS)) * g_ref[...]


def _out_proj_first(u, w, x, ctx, tab, tab_next, norm_g_next):
    nb, r, dw = u.shape
    s, l = x.shape[1], ctx.shape[1]
    d = w.shape[1]
    tm = FIRST_OUT_TILE if s % FIRST_OUT_TILE == 0 else ROW_TILE
    assert s % tm == 0 and l <= tm and s % l == 0
    nl = s // tm
    lat = lambda width: pl.BlockSpec((1, tm, width), lambda b, t: (b, jnp.minimum(t, nl - 1), 0))
    tabspec = lambda kind: pl.BlockSpec((1, 1, 8, d), lambda b, t: (b, kind, 0, 0))
    row = lambda b, t: (b, t, 0)
    return pl.pallas_call(
        _out_first_kernel,
        grid=(nb, nl + 1),
        in_specs=[lat(dw), pl.BlockSpec((1, l, dw), lambda b, t: (b, s // l, 0)),
                  pl.BlockSpec((dw, d), lambda b, t: (0, 0)),
                  lat(d), pl.BlockSpec((1, l, d), lambda b, t: (b, 0, 0)),
                  tabspec(0), tabspec(1), tabspec(0), tabspec(1),
                  pl.BlockSpec((1, d), lambda b, t: (0, 0))],
        out_specs=[pl.BlockSpec((1, tm, d), row), pl.BlockSpec((1, tm, d), row)],
        out_shape=[jax.ShapeDtypeStruct((nb, r, d), F32), jax.ShapeDtypeStruct((nb, r, d), BF16)],
        compiler_params=_cparams("parallel", "arbitrary"),
        name="out_proj_first",
    )(u, u, w, x, ctx, tab, tab, tab_next, tab_next, norm_g_next.reshape(1, d))


def _out_proj_final(u, w, x, tab, final_g, s):
    nb, r, d = x.shape
    dw = u.shape[2]
    row = lambda b, t: (b, t, 0)
    tm = FINAL_TILE if s % FINAL_TILE == 0 else ROW_TILE
    return pl.pallas_call(
        _out_final_kernel,
        grid=(nb, s // tm),
        in_specs=[pl.BlockSpec((1, tm, dw), row),
                  pl.BlockSpec((dw, d), lambda b, t: (0, 0)),
                  pl.BlockSpec((1, tm, d), row),
                  pl.BlockSpec((1, 1, 8, d), lambda b, t: (b, 0, 0, 0)),
                  pl.BlockSpec((1, d), lambda b, t: (0, 0))],
        out_specs=pl.BlockSpec((1, tm, d), row),
        out_shape=jax.ShapeDtypeStruct((nb, s, d), F32),
        compiler_params=_cparams("parallel", "parallel"),
        name="out_proj_final",
    )(u, w, x, tab, final_g.reshape(1, d))


def _halo_valid(nlat):
    t = pl.program_id(1)
    prev_ok = jnp.logical_and(t != 0, t != nlat)
    next_ok = jnp.logical_and(t != nlat - 1, t != nlat)
    return prev_ok.astype(F32), next_ok.astype(F32)


def _conf_out_kernel(pa_ref, ca_ref, na_ref, pb_ref, cb_ref, nb_ref, z_ref, w_ref, wb_ref, lg_ref, lb_ref,
                     wo_ref, x_ref, tg_ref, tn_ref, g_ref, xo_ref, ho_ref, win_ref, u_ref, *, nlat):
    tm = ca_ref.shape[1]
    nc = win_ref.shape[0]
    pv, nv = _halo_valid(nlat)

    def glu(a_ref, b_ref):
        return a_ref[0].astype(F32) * jax.nn.sigmoid(b_ref[0].astype(F32))

    gp, gc, gn = glu(pa_ref, pb_ref) * pv, glu(ca_ref, cb_ref), glu(na_ref, nb_ref) * nv
    for c in range(nc):
        sl = slice(c * LANES, (c + 1) * LANES)
        win_ref[c, 0:HALO, :] = gp[:, sl]
        win_ref[c, HALO:HALO + tm, :] = gc[:, sl]
        win_ref[c, HALO + tm:2 * HALO + tm, :] = gn[:, sl]

    off = HALO - CF_KERNEL // 2

    def conv_chunk(c, carry):
        w = w_ref[c]
        acc = jnp.zeros((tm, LANES), F32)
        for k in range(CF_KERNEL):
            acc = acc + win_ref[c, pl.ds(off + k, tm), :] * w[k:k + 1, :]
        u_ref[c] = acc + wb_ref[c][0:1]
        return carry

    lax.fori_loop(0, nc, conv_chunk, 0)

    d = nc * LANES
    s1 = u_ref[0]
    for c in range(1, nc):
        s1 = s1 + u_ref[c]
    mu = jnp.sum(s1, axis=-1, keepdims=True) / d
    s2 = jnp.square(u_ref[0] - mu)
    for c in range(1, nc):
        s2 = s2 + jnp.square(u_ref[c] - mu)
    rs = lax.rsqrt(jnp.sum(s2, axis=-1, keepdims=True) / d + LN_EPS)
    chunks = []
    for c in range(nc):
        sl = slice(c * LANES, (c + 1) * LANES)
        y = ((u_ref[c] - mu) * rs) * lg_ref[:, sl] + lb_ref[:, sl]
        zc = z_ref[0, :, sl].astype(F32)
        chunks.append((_silu(y) * _silu(zc)).astype(BF16))
    xo_ref[0], ho_ref[0] = _residual_out(jnp.concatenate(chunks, axis=1), wo_ref, x_ref[0], tg_ref[0, 0][2:3],
                                         tn_ref[0, 0], g_ref[...], k_major=True)


def _halo_specs(tm, d, col, r):
    nblk = tm // HALO
    last_blk = r // HALO - 1
    prev = pl.BlockSpec((1, HALO, d), lambda b, t: (b, jnp.maximum(t * nblk - 1, 0), col))
    cur = pl.BlockSpec((1, tm, d), lambda b, t: (b, t, col))
    nxt = pl.BlockSpec((1, HALO, d), lambda b, t: (b, jnp.minimum((t + 1) * nblk, last_blk), col))
    return [prev, cur, nxt]


def _out_specs_and_args(w_out, x, tab, tab_next, norm_g_next, nlat):
    nb, r, d = x.shape
    row = lambda b, t: (b, t, 0)
    tabspec = pl.BlockSpec((1, 1, 8, d), lambda b, t: (b, t // nlat, 0, 0))
    in_specs = [pl.BlockSpec(w_out.shape, lambda b, t: (0, 0)), pl.BlockSpec((1, ROW_TILE, d), row),
                tabspec, tabspec, pl.BlockSpec((1, d), lambda b, t: (0, 0))]
    args = [w_out, x, tab, tab_next, norm_g_next.reshape(1, d)]
    out_specs = [pl.BlockSpec((1, ROW_TILE, d), row), pl.BlockSpec((1, ROW_TILE, d), row)]
    out_shape = [jax.ShapeDtypeStruct((nb, r, d), F32), jax.ShapeDtypeStruct((nb, r, d), BF16)]
    return in_specs, args, out_specs, out_shape


def _conformer_layer(abz, p, w_out, x, tab, tab_next, norm_g_next, s, l):
    nb, r, d3 = abz.shape
    d = d3 // 3
    tm = ROW_TILE
    assert l == tm and s % tm == 0 and d % LANES == 0
    nc = d // LANES
    nlat = s // tm
    specs = _halo_specs(tm, d, 0, r) + _halo_specs(tm, d, 1, r)
    z_spec = pl.BlockSpec((1, tm, d), lambda b, t: (b, t, 2))
    w = jnp.pad(p["dw_w"].astype(F32), ((0, 1), (0, 0))).reshape(CF_KERNEL + 1, nc, LANES).transpose(1, 0, 2)
    wb = jnp.broadcast_to(p["dw_b"].astype(F32).reshape(nc, 1, LANES), (nc, 8, LANES))
    full = lambda shape: pl.BlockSpec(shape, lambda b, t: (0,) * len(shape))
    o_in, o_args, out_specs, out_shape = _out_specs_and_args(w_out, x, tab, tab_next, norm_g_next, nlat)
    return pl.pallas_call(
        functools.partial(_conf_out_kernel, nlat=nlat),
        grid=(nb, r // tm),
        in_specs=specs + [z_spec, full((nc, CF_KERNEL + 1, LANES)), full((nc, 8, LANES)),
                          full((1, d)), full((1, d))] + o_in,
        out_specs=out_specs,
        out_shape=out_shape,
        scratch_shapes=[pltpu.VMEM((nc, tm + 2 * HALO, LANES), F32), pltpu.VMEM((nc, tm, LANES), F32)],
        compiler_params=_cparams("parallel", "parallel"),
        name="conformer_layer",
    )(abz, abz, abz, abz, abz, abz, abz, w, wb, p["ln_g"].astype(F32).reshape(1, d),
      p["ln_b"].astype(F32).reshape(1, d), *o_args)


def _sc_out_kernel(bg_ref, pc_ref, cc_ref, nc_ref, pv_ref, cv_ref, nv_ref, z_ref, w_ref,
                   wo_ref, x_ref, tg_ref, tn_ref, g_ref, xo_ref, ho_ref, win_ref, *, nlat):
    tm = cc_ref.shape[1]
    nc = win_ref.shape[0]
    pvalid, nvalid = _halo_valid(nlat)

    def prod(c_ref, v_ref):
        return c_ref[0].astype(F32) * v_ref[0].astype(F32)

    wp, wc, wn = prod(pc_ref, pv_ref) * pvalid, prod(cc_ref, cv_ref), prod(nc_ref, nv_ref) * nvalid
    off = HALO - SC_KERNEL // 2
    w = w_ref[...]
    chunks = []
    for c in range(nc):
        sl = slice(c * LANES, (c + 1) * LANES)
        win_ref[c, 0:HALO, :] = wp[:, sl]
        win_ref[c, HALO:HALO + tm, :] = wc[:, sl]
        win_ref[c, HALO + tm:2 * HALO + tm, :] = wn[:, sl]
        y = win_ref[c, pl.ds(off, tm), :] * w[0:1, sl]
        for k in range(1, SC_KERNEL):
            y = y + win_ref[c, pl.ds(off + k, tm), :] * w[k:k + 1, sl]
        chunks.append(((bg_ref[0, :, sl].astype(F32) * y) * _silu(z_ref[0, :, sl].astype(F32))).astype(BF16))
    u = jnp.concatenate(chunks, axis=1)
    xo_ref[0], ho_ref[0] = _residual_out(u, wo_ref, x_ref[0], tg_ref[0, 0][2:3], tn_ref[0, 0], g_ref[...],
                                         k_major=True)


def _shortconv_layer(bcvz, p, w_out, x, tab, tab_next, norm_g_next, s, l):
    nb, r, d4 = bcvz.shape
    d = d4 // 4
    tm = ROW_TILE
    assert l == tm and s % tm == 0
    nlat = s // tm
    specs = [pl.BlockSpec((1, tm, d), lambda b, t: (b, t, 0))]
    specs += _halo_specs(tm, d, 1, r) + _halo_specs(tm, d, 2, r)
    specs.append(pl.BlockSpec((1, tm, d), lambda b, t: (b, t, 3)))
    specs.append(pl.BlockSpec((8, d), lambda b, t: (0, 0)))
    w = jnp.pad(p["conv_w"].astype(F32), ((0, 8 - SC_KERNEL), (0, 0)))
    o_in, o_args, out_specs, out_shape = _out_specs_and_args(w_out, x, tab, tab_next, norm_g_next, nlat)
    return pl.pallas_call(
        functools.partial(_sc_out_kernel, nlat=nlat),
        grid=(nb, r // tm),
        in_specs=specs + o_in,
        out_specs=out_specs,
        out_shape=out_shape,
        scratch_shapes=[pltpu.VMEM((d // LANES, tm + 2 * HALO, LANES), F32)],
        compiler_params=_cparams("parallel", "parallel"),
        name="shortconv_layer",
    )(bcvz, bcvz, bcvz, bcvz, bcvz, bcvz, bcvz, bcvz, w, *o_args)


def _forward(x, c, ctx, c_ctx, layers, final_norm_g):
    nb, s, d = x.shape
    l = ctx.shape[1]
    r = s + l
    assert l == ROW_TILE and s % ROW_TILE == 0 and nb < 8
    depth = len(layers)
    kinds = ("attn", "conformer", "shortconv")

    cond8 = jnp.zeros((8, d), F32).at[:nb].set(c).at[nb].set(c_ctx)
    tabs = [_mod_table(_adaln(cond8, p["ada_w"], p["ada_b"]), nb, d) for p in layers]
    rope = _rope_tables(s, l)

    h = _prologue(x, ctx, tabs[0], layers[0]["norm_g"])
    xs = None
    for i, p in enumerate(layers):
        kind = kinds[i % len(kinds)]
        last = i == depth - 1
        w_in = p["w_in"]
        w_out = p["w_out"].astype(BF16)
        h2 = h.reshape(nb * r, d)
        if kind == "attn":
            qkvz = _in_proj(h2, w_in, r, rope=rope, rope_cols=w_in.shape[1] // 2).reshape(nb, r, -1)
            u = _diff_attention(qkvz, p, i, s, l, keep_ctx=not last)
            if last:
                return _out_proj_final(u, w_out, xs, tabs[i], final_norm_g, s)
            assert i == 0, "attention layers sit at the two ends of the stack"
            xs, h = _out_proj_first(u, w_out, x, ctx, tabs[i], tabs[i + 1], layers[i + 1]["norm_g"])
        else:
            assert not last and i > 0, "the stack starts and ends with an attention layer"
            proj = _in_proj(h2, w_in, r).reshape(nb, r, -1)
            layer = _conformer_layer if kind == "conformer" else _shortconv_layer
            xs, h = layer(proj, p, w_out, xs, tabs[i], tabs[i + 1], layers[i + 1]["norm_g"], s, l)


def kernel(x, c, ctx, c_ctx, l0_norm_g, l0_ada_w, l0_ada_b, l0_w_in, l0_lam_q1, l0_lam_k1, l0_lam_q2, l0_lam_k2, l0_head_g, l0_w_out, l1_norm_g, l1_ada_w, l1_ada_b, l1_w_in, l1_dw_w, l1_dw_b, l1_ln_g, l1_ln_b, l1_w_out, l2_norm_g, l2_ada_w, l2_ada_b, l2_w_in, l2_conv_w, l2_w_out, l3_norm_g, l3_ada_w, l3_ada_b, l3_w_in, l3_lam_q1, l3_lam_k1, l3_lam_q2, l3_lam_k2, l3_head_g, l3_w_out, final_norm_g):
    layers = [
        dict(norm_g=l0_norm_g, ada_w=l0_ada_w, ada_b=l0_ada_b, w_in=l0_w_in, lam_q1=l0_lam_q1, lam_k1=l0_lam_k1,
             lam_q2=l0_lam_q2, lam_k2=l0_lam_k2, head_g=l0_head_g, w_out=l0_w_out),
        dict(norm_g=l1_norm_g, ada_w=l1_ada_w, ada_b=l1_ada_b, w_in=l1_w_in, dw_w=l1_dw_w, dw_b=l1_dw_b,
             ln_g=l1_ln_g, ln_b=l1_ln_b, w_out=l1_w_out),
        dict(norm_g=l2_norm_g, ada_w=l2_ada_w, ada_b=l2_ada_b, w_in=l2_w_in, conv_w=l2_conv_w, w_out=l2_w_out),
        dict(norm_g=l3_norm_g, ada_w=l3_ada_w, ada_b=l3_ada_b, w_in=l3_w_in, lam_q1=l3_lam_q1, lam_k1=l3_lam_k1,
             lam_q2=l3_lam_q2, lam_k2=l3_lam_k2, head_g=l3_head_g, w_out=l3_w_out),
    ]
    return _forward(x, c, ctx, c_ctx, layers, final_norm_g)
```

```python
import functools
import math

import jax
import jax.numpy as jnp
from jax import lax
from jax.experimental import pallas as pl
from jax.experimental.pallas import tpu as pltpu

F32 = jnp.float32
BF16 = jnp.bfloat16

LANES = 128
HALO = 16
ROW_TILE = 256
DA_HEAD_DIM = 64
HEAD_W = 2 * DA_HEAD_DIM
ROPE_BASE = 10000.0
ROPE_FREQS = DA_HEAD_DIM // 4
GRID_W = 64
CF_KERNEL = 31
SC_KERNEL = 3
NORM_EPS = 1e-6
LN_EPS = 1e-5
Q_TILE = 2048
Q_SUB = 512
FINAL_TILE = 512
FIRST_OUT_TILE = 512
KV_CHUNK = 256
Q_SCALE = math.log2(math.e) / math.sqrt(DA_HEAD_DIM)
VMEM_LIMIT = 52 * 1024 * 1024


def _cparams(*sem):
    return pltpu.CompilerParams(dimension_semantics=sem, vmem_limit_bytes=VMEM_LIMIT)


def _silu(x):
    return x * jax.nn.sigmoid(x)


def _ada_kernel(c_ref, w_ref, b_ref, o_ref):
    a = _silu(c_ref[...]).astype(BF16)
    o_ref[...] = jnp.dot(a, w_ref[...].astype(BF16), preferred_element_type=F32) + b_ref[...]


def _adaln(cond8, w, b):
    d, n = w.shape
    tn = min(1024, n)
    return pl.pallas_call(
        _ada_kernel,
        grid=(n // tn,),
        in_specs=[pl.BlockSpec((8, d), lambda j: (0, 0)),
                  pl.BlockSpec((d, tn), lambda j: (0, j)),
                  pl.BlockSpec((1, tn), lambda j: (0, j))],
        out_specs=pl.BlockSpec((8, tn), lambda j: (0, j)),
        out_shape=jax.ShapeDtypeStruct((8, n), F32),
        compiler_params=_cparams("parallel"),
        name="adaln",
    )(cond8, w, b.reshape(1, n))


def _mod_table(m, nb, d):
    sh, sc, g = m[:, :d], m[:, d:2 * d], m[:, 2 * d:]
    rows = jnp.stack([1.0 + sc, sh, g], axis=1)
    lat = rows[:nb]
    ctx = jnp.broadcast_to(rows[nb][None], (nb, 3, d))
    tab = jnp.stack([lat, ctx], axis=1)
    return jnp.pad(tab, ((0, 0), (0, 0), (0, 5), (0, 0)))


def _norm_mod(x, g, scale1p, shift):
    ms = jnp.mean(x * x, axis=-1, keepdims=True)
    y = x * lax.rsqrt(ms + NORM_EPS)
    return (y * g) * scale1p + shift


def _pro_kernel(x_ref, c_ref, tl_ref, tc_ref, g_ref, h_ref, *, n_lat_tail):
    last = pl.num_programs(1) - 1
    g = g_ref[...]
    tl = tl_ref[0, 0]

    @pl.when(pl.program_id(1) < last)
    def _():
        h_ref[0] = _norm_mod(x_ref[0], g, tl[0:1], tl[1:2]).astype(BF16)

    @pl.when(pl.program_id(1) == last)
    def _():
        tc = tc_ref[0, 0]
        h_ref[0, 0:n_lat_tail, :] = _norm_mod(x_ref[0, 0:n_lat_tail, :], g, tl[0:1], tl[1:2]).astype(BF16)
        h_ref[0, n_lat_tail:, :] = _norm_mod(c_ref[0], g, tc[0:1], tc[1:2]).astype(BF16)


def _prologue(x, ctx, tab, norm_g):
    nb, s, d = x.shape
    l = ctx.shape[1]
    r = s + l
    tm = r // 4
    n_lat_tail = tm - l
    assert r % 4 == 0 and tm > l and n_lat_tail % HALO == 0 and s == 3 * tm + n_lat_tail
    tabspec = lambda kind: pl.BlockSpec((1, 1, 8, d), lambda b, t: (b, kind, 0, 0))
    return pl.pallas_call(
        functools.partial(_pro_kernel, n_lat_tail=n_lat_tail),
        grid=(nb, 4),
        in_specs=[pl.BlockSpec((1, tm, d), lambda b, t: (b, t, 0)),
                  pl.BlockSpec((1, l, d), lambda b, t: (b, 0, 0)),
                  tabspec(0), tabspec(1),
                  pl.BlockSpec((1, d), lambda b, t: (0, 0))],
        out_specs=pl.BlockSpec((1, tm, d), lambda b, t: (b, t, 0)),
        out_shape=jax.ShapeDtypeStruct((nb, r, d), BF16),
        compiler_params=_cparams("parallel", "arbitrary"),
        name="prologue",
    )(x, ctx, tab, tab, norm_g.reshape(1, d))


def _in_kernel(h_ref, w_ref, o_ref, wb_ref):
    @pl.when(pl.program_id(1) == 0)
    def _():
        wb_ref[...] = w_ref[...].astype(BF16)

    o_ref[...] = jnp.dot(h_ref[...], wb_ref[...], preferred_element_type=F32).astype(BF16)


def _in_rope_kernel(h_ref, w_ref, cos_ref, sa_ref, sb_ref, o_ref, wb_ref, *, n_rope, tn, sub):
    j = pl.program_id(0)
    tm = h_ref.shape[0]

    @pl.when(pl.program_id(1) == 0)
    def _():
        wb_ref[...] = w_ref[...].astype(BF16)

    @pl.when(j < n_rope)
    def _():
        for r0 in range(0, tm, sub):
            rows = slice(r0, r0 + sub)
            acc = jnp.dot(h_ref[rows, :], wb_ref[...], preferred_element_type=F32)
            cos, sa, sb = cos_ref[rows, :], sa_ref[rows, :], sb_ref[rows, :]
            for c in range(tn // LANES):
                xc = acc[:, c * LANES:(c + 1) * LANES]
                rot = (xc * cos + pltpu.roll(xc, LANES - ROPE_FREQS, 1) * sa
                       + pltpu.roll(xc, ROPE_FREQS, 1) * sb)
                o_ref[rows, c * LANES:(c + 1) * LANES] = rot.astype(BF16)

    @pl.when(j >= n_rope)
    def _():
        o_ref[...] = jnp.dot(h_ref[...], wb_ref[...], preferred_element_type=F32).astype(BF16)


def _in_tiles(t, r, d, n):
    tm = r // 4
    tn = 1024 if d >= 1024 else d
    assert tm % HALO == 0 and t % tm == 0 and n % tn == 0
    return tm, tn


def _in_proj(h, w, r, rope=None, rope_cols=0):
    t, d = h.shape
    n = w.shape[1]
    tm, tn = _in_tiles(t, r, d, n)
    grid = (n // tn, t // tm)
    h_spec = pl.BlockSpec((tm, d), lambda j, i: (i, 0))
    w_spec = pl.BlockSpec((d, tn), lambda j, i: (0, j))
    o_spec = pl.BlockSpec((tm, tn), lambda j, i: (i, j))
    out_shape = jax.ShapeDtypeStruct((t, n), BF16)
    scratch = [pltpu.VMEM((d, tn), BF16)]
    if rope is None:
        return pl.pallas_call(_in_kernel, grid=grid, in_specs=[h_spec, w_spec], out_specs=o_spec,
                              out_shape=out_shape, scratch_shapes=scratch,
                              compiler_params=_cparams("parallel", "arbitrary"), name="in_proj")(h, w)
    assert rope_cols % (2 * tn) == 0 and tn % LANES == 0
    per = r // tm
    n_q = rope_cols // (2 * tn)
    t_spec = pl.BlockSpec((tm, LANES), lambda j, i: (i % per + jnp.where(j >= n_q, per, 0), 0))
    sub = tm // 4 if (tm // 4) % HALO == 0 else tm
    kern = functools.partial(_in_rope_kernel, n_rope=rope_cols // tn, tn=tn, sub=sub)
    return pl.pallas_call(kern, grid=grid, in_specs=[h_spec, w_spec, t_spec, t_spec, t_spec],
                          out_specs=o_spec, out_shape=out_shape, scratch_shapes=scratch,
                          compiler_params=_cparams("parallel", "arbitrary"),
                          name="in_proj_rope")(h, w, *rope)


def _rope_tables(s, l):
    t = jnp.arange(s)
    row = (t // GRID_W).astype(F32)
    col = (t % GRID_W).astype(F32)
    inv = ROPE_BASE ** (-jnp.arange(ROPE_FREQS, dtype=F32) / ROPE_FREQS)
    lane = jnp.arange(LANES)
    dd = lane % DA_HEAD_DIM
    axis, half, f = dd // (2 * ROPE_FREQS), (dd % (2 * ROPE_FREQS)) // ROPE_FREQS, dd % ROPE_FREQS
    pos = jnp.where(axis[None, :] == 0, row[:, None], col[:, None])
    ang = pos * inv[f][None, :]
    cos, sin = jnp.cos(ang), jnp.sin(ang)
    sa = jnp.where(half[None, :] == 0, -sin, 0.0)
    sb = jnp.where(half[None, :] == 1, sin, 0.0)
    cos = jnp.concatenate([cos, jnp.ones((l, LANES), F32)], axis=0)
    sa = jnp.concatenate([sa, jnp.zeros((l, LANES), F32)], axis=0)
    sb = jnp.concatenate([sb, jnp.zeros((l, LANES), F32)], axis=0)
    return tuple(jnp.concatenate([t * Q_SCALE, t], axis=0) for t in (cos, sa, sb))


V_ROWS = HEAD_W + HALO


SAFE_DENOM = 2.0 ** -100
BOUND_SLACK = 1.01


def _split_comps(q):
    lane = lax.broadcasted_iota(jnp.int32, q.shape, 1)
    return jnp.concatenate([jnp.where((lane // DA_HEAD_DIM) == c, q, jnp.zeros_like(q)) for c in range(2)], axis=0)


def _nt_dot(a, b):
    return lax.dot_general(a, b, (((1,), (1,)), ((), ())), preferred_element_type=F32)


def _exact_chunks(k_ref, vt_ref, q01, chunks, ck):
    def scores(j):
        return _nt_dot(k_ref[0, j * ck:(j + 1) * ck, :], q01)

    s_next = scores(chunks[0])
    m = acc = None
    for i, j in enumerate(chunks):
        s = s_next
        if i + 1 < len(chunks):
            s_next = scores(chunks[i + 1])
        mc = jnp.max(s, axis=0, keepdims=True)
        m_new = mc if m is None else jnp.maximum(m, mc)
        p = jnp.exp2(s - m_new).astype(BF16)
        pv = jnp.dot(vt_ref[:, j * ck:(j + 1) * ck], p, preferred_element_type=F32)
        acc = pv if m is None else acc * jnp.exp2(m - m_new) + pv
        m = m_new
    return acc


def _bound_chunks(k_ref, vt_ref, q01, mb, chunks, ck):
    def scores(j):
        return _nt_dot(k_ref[0, j * ck:(j + 1) * ck, :], q01)

    acc = None
    s_next = scores(chunks[0])
    for i, j in enumerate(chunks):
        s = s_next
        if i + 1 < len(chunks):
            s_next = scores(chunks[i + 1])
        p = jnp.exp2(s - mb).astype(BF16)
        pv = jnp.dot(vt_ref[:, j * ck:(j + 1) * ck], p, preferred_element_type=F32)
        acc = pv if acc is None else acc + pv
    return acc


def _attn_finish(acc, lam, hg, z, lam_init):
    t = acc.shape[1] // 2
    o = acc[0:HEAD_W] * (1.0 / acc[HEAD_W:HEAD_W + 1])
    d = o[:, 0:t] - lam * o[:, t:2 * t]
    ms = jnp.mean(d * d, axis=0, keepdims=True)
    dn = (d * lax.rsqrt(ms + NORM_EPS)).T
    on = (dn * hg) * (1.0 - lam_init)
    return (on * _silu(z.astype(F32))).astype(BF16)


def _attn_body(lam_ref, hg_ref, q_ref, qc_ref, k_ref, v_ref, z_ref, zc_ref, o_ref, vt_ref, kn_ref, *,
               lam_init, ck, sub, keep_ctx):
    qi = pl.program_id(2)
    nkv = k_ref.shape[1]
    nchunk = nkv // ck

    @pl.when(qi == 0)
    def _():
        vt_ref[0:HEAD_W, :] = v_ref[0].astype(F32).T.astype(BF16)
        vt_ref[HEAD_W:V_ROWS, :] = jnp.ones((HALO, nkv), BF16)
        kf = k_ref[0].astype(F32)
        r_i = lax.broadcasted_iota(jnp.int32, (HEAD_W, HEAD_W), 0) // DA_HEAD_DIM
        c_i = lax.broadcasted_iota(jnp.int32, (HEAD_W, HEAD_W), 1) // DA_HEAD_DIM
        sel = jnp.where(r_i == c_i, 1.0, 0.0).astype(BF16)
        kn2 = jnp.dot((kf * kf).astype(BF16), sel, preferred_element_type=F32)
        kn_ref[...] = jnp.broadcast_to(jnp.sqrt(jnp.max(kn2, axis=0, keepdims=True)), kn_ref.shape)

    lp = lam_ref[...]
    lam = (jnp.exp(jnp.sum(lp[0:1] * lp[1:2], axis=-1, keepdims=True))
           - jnp.exp(jnp.sum(lp[2:3] * lp[3:4], axis=-1, keepdims=True)) + lam_init)
    hg = hg_ref[...]

    if keep_ctx:
        @pl.when(qi == 0)
        def _():
            acc = _exact_chunks(k_ref, vt_ref, _split_comps(qc_ref[0]), [nchunk - 1], ck)
            o_ref[0, 0:qc_ref.shape[1], :] = _attn_finish(acc, lam, hg, zc_ref[0], lam_init)

    @pl.when(qi >= (1 if keep_ctx else 0))
    def _():
        nsub = q_ref.shape[1] // sub
        kn = kn_ref[0:1, :]
        kmax = jnp.concatenate([jnp.broadcast_to(kn[:, 0:1], (1, sub)),
                                jnp.broadcast_to(kn[:, DA_HEAD_DIM:DA_HEAD_DIM + 1], (1, sub))], axis=1)
        denoms = []
        for t in range(nsub):
            rows = slice(t * sub, (t + 1) * sub)
            q01 = _split_comps(q_ref[0, rows, :])
            qf = q01.astype(F32)
            qn2 = _nt_dot(jnp.ones((HALO, HEAD_W), BF16), (qf * qf).astype(BF16))[0:1]
            mb = jnp.sqrt(qn2) * kmax * BOUND_SLACK
            acc = _bound_chunks(k_ref, vt_ref, q01, mb, range(nchunk), ck)
            o_ref[0, rows, :] = _attn_finish(acc, lam, hg, z_ref[0, rows, :], lam_init)
            denoms.append(jnp.min(acc[HEAD_W:HEAD_W + 1]))
        safe = functools.reduce(jnp.minimum, denoms) >= SAFE_DENOM

        @pl.when(jnp.logical_not(safe))
        def _():
            def redo(t, carry):
                rows = pl.ds(pl.multiple_of(t * sub, sub), sub)
                acc2 = _exact_chunks(k_ref, vt_ref, _split_comps(q_ref[0, rows, :]), range(nchunk), ck)
                o_ref[0, rows, :] = _attn_finish(acc2, lam, hg, z_ref[0, rows, :], lam_init)
                return carry

            lax.fori_loop(0, nsub, redo, 0)


def _diff_attention(qkvz, p, layer_idx, s, l, keep_ctx):
    nb, r, n4 = qkvz.shape
    nh = n4 // (4 * HEAD_W)
    lam_init = 0.8 - 0.6 * math.exp(-0.3 * layer_idx)
    lamp = jnp.stack([p["lam_q1"], p["lam_k1"], p["lam_q2"], p["lam_k2"]]).astype(F32)
    lamp = jnp.pad(lamp, ((0, 4), (0, LANES - DA_HEAD_DIM)))
    hg = p["head_g"].astype(F32).reshape(1, HEAD_W)
    tq = min(Q_TILE, s)
    ck = KV_CHUNK
    sub = min(Q_SUB, tq)
    assert s % tq == 0 and tq % sub == 0 and l == ck and r % ck == 0
    nlat_q = s // tq
    n_pre = 1 if keep_ctx else 0
    ctx_blk = s // l
    kern = functools.partial(_attn_body, lam_init=lam_init, ck=ck, sub=sub, keep_ctx=keep_ctx)
    const = lambda b, h, i: (0, 0)
    lat = lambda col: pl.BlockSpec((1, tq, HEAD_W),
                                   lambda b, h, i: (b, jnp.maximum(i - n_pre, 0), col * nh + h))
    ctx = lambda col: pl.BlockSpec((1, l, HEAD_W), lambda b, h, i: (b, ctx_blk, col * nh + h))
    kv = lambda col: pl.BlockSpec((1, r, HEAD_W), lambda b, h, i: (b, 0, col * nh + h))
    out_map = lambda b, h, i: (b, jnp.where(i < n_pre, nlat_q, i - n_pre), h)
    return pl.pallas_call(
        kern,
        grid=(nb, nh, nlat_q + n_pre),
        in_specs=[pl.BlockSpec((8, LANES), const), pl.BlockSpec((1, LANES), const),
                  lat(0), ctx(0), kv(1), kv(2), lat(3), ctx(3)],
        out_specs=pl.BlockSpec((1, tq, HEAD_W), out_map),
        out_shape=jax.ShapeDtypeStruct((nb, r if keep_ctx else s, nh * HEAD_W), BF16),
        scratch_shapes=[pltpu.VMEM((V_ROWS, r), BF16), pltpu.VMEM((8, LANES), F32)],
        compiler_params=_cparams("parallel", "parallel", "arbitrary"),
        name="diff_attn",
    )(lamp, hg, qkvz, qkvz, qkvz, qkvz, qkvz, qkvz)


MXU_K = 256


def _residual_out(u, w_ref, x, gate, tab_next, g_next, k_major=False):
    if k_major:
        y = None
        for k0 in range(0, u.shape[1], MXU_K):
            part = jnp.dot(u[:, k0:k0 + MXU_K], w_ref[k0:k0 + MXU_K, :], preferred_element_type=F32)
            y = part if y is None else y + part
    else:
        y = jnp.dot(u, w_ref[...], preferred_element_type=F32)
    xn = x + gate * y
    return xn, _norm_mod(xn, g_next, tab_next[0:1], tab_next[1:2]).astype(BF16)


def _out_first_kernel(ul_ref, uc_ref, w_ref, xl_ref, xc_ref, tgl_ref, tgc_ref, tnl_ref, tnc_ref, g_ref,
                      xo_ref, ho_ref):
    last = pl.num_programs(1) - 1
    g = g_ref[...]

    @pl.when(pl.program_id(1) < last)
    def _():
        xn, hn = _residual_out(ul_ref[0], w_ref, xl_ref[0], tgl_ref[0, 0][2:3], tnl_ref[0, 0], g)
        xo_ref[0] = xn
        ho_ref[0] = hn

    @pl.when(pl.program_id(1) == last)
    def _():
        n = uc_ref.shape[1]
        xn, hn = _residual_out(uc_ref[0], w_ref, xc_ref[0], tgc_ref[0, 0][2:3], tnc_ref[0, 0], g)
        xo_ref[0, 0:n, :] = xn
        ho_ref[0, 0:n, :] = hn


def _out_final_kernel(u_ref, w_ref, x_ref, tg_ref, g_ref, o_ref):
    y = jnp.dot(u_ref[0], w_ref[...], preferred_element_type=F32)
    xn = x_ref[0] + tg_ref[0, 0][2:3] * y
    ms = jnp.mean(xn * xn, axis=-1, keepdims=True)
    o_ref[0] = (xn * lax.rsqrt(ms + NORM_EPS)) * g_ref[...]


def _out_proj_first(u, w, x, ctx, tab, tab_next, norm_g_next):
    nb, r, dw = u.shape
    s, l = x.shape[1], ctx.shape[1]
    d = w.shape[1]
    tm = FIRST_OUT_TILE if s % FIRST_OUT_TILE == 0 else ROW_TILE
    assert s % tm == 0 and l <= tm and s % l == 0
    nl = s // tm
    lat = lambda width: pl.BlockSpec((1, tm, width), lambda b, t: (b, jnp.minimum(t, nl - 1), 0))
    tabspec = lambda kind: pl.BlockSpec((1, 1, 8, d), lambda b, t: (b, kind, 0, 0))
    row = lambda b, t: (b, t, 0)
    return pl.pallas_call(
        _out_first_kernel,
        grid=(nb, nl + 1),
        in_specs=[lat(dw), pl.BlockSpec((1, l, dw), lambda b, t: (b, s // l, 0)),
                  pl.BlockSpec((dw, d), lambda b, t: (0, 0)),
                  lat(d), pl.BlockSpec((1, l, d), lambda b, t: (b, 0, 0)),
                  tabspec(0), tabspec(1), tabspec(0), tabspec(1),
                  pl.BlockSpec((1, d), lambda b, t: (0, 0))],
        out_specs=[pl.BlockSpec((1, tm, d), row), pl.BlockSpec((1, tm, d), row)],
        out_shape=[jax.ShapeDtypeStruct((nb, r, d), F32), jax.ShapeDtypeStruct((nb, r, d), BF16)],
        compiler_params=_cparams("parallel", "arbitrary"),
        name="out_proj_first",
    )(u, u, w, x, ctx, tab, tab, tab_next, tab_next, norm_g_next.reshape(1, d))


def _out_proj_final(u, w, x, tab, final_g, s):
    nb, r, d = x.shape
    dw = u.shape[2]
    row = lambda b, t: (b, t, 0)
    tm = FINAL_TILE if s % FINAL_TILE == 0 else ROW_TILE
    return pl.pallas_call(
        _out_final_kernel,
        grid=(nb, s // tm),
        in_specs=[pl.BlockSpec((1, tm, dw), row),
                  pl.BlockSpec((dw, d), lambda b, t: (0, 0)),
                  pl.BlockSpec((1, tm, d), row),
                  pl.BlockSpec((1, 1, 8, d), lambda b, t: (b, 0, 0, 0)),
                  pl.BlockSpec((1, d), lambda b, t: (0, 0))],
        out_specs=pl.BlockSpec((1, tm, d), row),
        out_shape=jax.ShapeDtypeStruct((nb, s, d), F32),
        compiler_params=_cparams("parallel", "parallel"),
        name="out_proj_final",
    )(u, w, x, tab, final_g.reshape(1, d))


def _halo_valid(nlat):
    t = pl.program_id(1)
    prev_ok = jnp.logical_and(t != 0, t != nlat)
    next_ok = jnp.logical_and(t != nlat - 1, t != nlat)
    return prev_ok.astype(F32), next_ok.astype(F32)


def _conf_out_kernel(pa_ref, ca_ref, na_ref, pb_ref, cb_ref, nb_ref, z_ref, w_ref, wb_ref, lg_ref, lb_ref,
                     wo_ref, x_ref, tg_ref, tn_ref, g_ref, xo_ref, ho_ref, win_ref, u_ref, *, nlat):
    tm = ca_ref.shape[1]
    nc = win_ref.shape[0]
    pv, nv = _halo_valid(nlat)

    for c in range(nc):
        sl = slice(c * LANES, (c + 1) * LANES)

        def glu(a_ref, b_ref):
            return a_ref[0, :, sl].astype(F32) * jax.nn.sigmoid(b_ref[0, :, sl].astype(F32))

        win_ref[c, 0:HALO, :] = glu(pa_ref, pb_ref) * pv
        win_ref[c, HALO:HALO + tm, :] = glu(ca_ref, cb_ref)
        win_ref[c, HALO + tm:2 * HALO + tm, :] = glu(na_ref, nb_ref) * nv

    off = HALO - CF_KERNEL // 2

    def conv_chunk(c, carry):
        w = w_ref[c]
        acc = jnp.zeros((tm, LANES), F32)
        for k in range(CF_KERNEL):
            acc = acc + win_ref[c, pl.ds(off + k, tm), :] * w[k:k + 1, :]
        u_ref[c] = acc + wb_ref[c][0:1]
        return carry

    lax.fori_loop(0, nc, conv_chunk, 0)

    d = nc * LANES
    s1 = u_ref[0]
    for c in range(1, nc):
        s1 = s1 + u_ref[c]
    mu = jnp.sum(s1, axis=-1, keepdims=True) / d
    s2 = jnp.square(u_ref[0] - mu)
    for c in range(1, nc):
        s2 = s2 + jnp.square(u_ref[c] - mu)
    rs = lax.rsqrt(jnp.sum(s2, axis=-1, keepdims=True) / d + LN_EPS)
    chunks = []
    for c in range(nc):
        sl = slice(c * LANES, (c + 1) * LANES)
        y = ((u_ref[c] - mu) * rs) * lg_ref[:, sl] + lb_ref[:, sl]
        zc = z_ref[0, :, sl].astype(F32)
        chunks.append((_silu(y) * _silu(zc)).astype(BF16))
    xo_ref[0], ho_ref[0] = _residual_out(jnp.concatenate(chunks, axis=1), wo_ref, x_ref[0], tg_ref[0, 0][2:3],
                                         tn_ref[0, 0], g_ref[...], k_major=True)


def _halo_specs(tm, d, col, r):
    nblk = tm // HALO
    last_blk = r // HALO - 1
    prev = pl.BlockSpec((1, HALO, d), lambda b, t: (b, jnp.maximum(t * nblk - 1, 0), col))
    cur = pl.BlockSpec((1, tm, d), lambda b, t: (b, t, col))
    nxt = pl.BlockSpec((1, HALO, d), lambda b, t: (b, jnp.minimum((t + 1) * nblk, last_blk), col))
    return [prev, cur, nxt]


def _out_specs_and_args(w_out, x, tab, tab_next, norm_g_next, nlat):
    nb, r, d = x.shape
    row = lambda b, t: (b, t, 0)
    tabspec = pl.BlockSpec((1, 1, 8, d), lambda b, t: (b, t // nlat, 0, 0))
    in_specs = [pl.BlockSpec(w_out.shape, lambda b, t: (0, 0)), pl.BlockSpec((1, ROW_TILE, d), row),
                tabspec, tabspec, pl.BlockSpec((1, d), lambda b, t: (0, 0))]
    args = [w_out, x, tab, tab_next, norm_g_next.reshape(1, d)]
    out_specs = [pl.BlockSpec((1, ROW_TILE, d), row), pl.BlockSpec((1, ROW_TILE, d), row)]
    out_shape = [jax.ShapeDtypeStruct((nb, r, d), F32), jax.ShapeDtypeStruct((nb, r, d), BF16)]
    return in_specs, args, out_specs, out_shape


def _conformer_layer(abz, p, w_out, x, tab, tab_next, norm_g_next, s, l):
    nb, r, d3 = abz.shape
    d = d3 // 3
    tm = ROW_TILE
    assert l == tm and s % tm == 0 and d % LANES == 0
    nc = d // LANES
    nlat = s // tm
    specs = _halo_specs(tm, d, 0, r) + _halo_specs(tm, d, 1, r)
    z_spec = pl.BlockSpec((1, tm, d), lambda b, t: (b, t, 2))
    w = jnp.pad(p["dw_w"].astype(F32), ((0, 1), (0, 0))).reshape(CF_KERNEL + 1, nc, LANES).transpose(1, 0, 2)
    wb = jnp.broadcast_to(p["dw_b"].astype(F32).reshape(nc, 1, LANES), (nc, 8, LANES))
    full = lambda shape: pl.BlockSpec(shape, lambda b, t: (0,) * len(shape))
    o_in, o_args, out_specs, out_shape = _out_specs_and_args(w_out, x, tab, tab_next, norm_g_next, nlat)
    return pl.pallas_call(
        functools.partial(_conf_out_kernel, nlat=nlat),
        grid=(nb, r // tm),
        in_specs=specs + [z_spec, full((nc, CF_KERNEL + 1, LANES)), full((nc, 8, LANES)),
                          full((1, d)), full((1, d))] + o_in,
        out_specs=out_specs,
        out_shape=out_shape,
        scratch_shapes=[pltpu.VMEM((nc, tm + 2 * HALO, LANES), F32), pltpu.VMEM((nc, tm, LANES), F32)],
        compiler_params=_cparams("parallel", "parallel"),
        name="conformer_layer",
    )(abz, abz, abz, abz, abz, abz, abz, w, wb, p["ln_g"].astype(F32).reshape(1, d),
      p["ln_b"].astype(F32).reshape(1, d), *o_args)


def _sc_out_kernel(bg_ref, pc_ref, cc_ref, nc_ref, pv_ref, cv_ref, nv_ref, z_ref, w_ref,
                   wo_ref, x_ref, tg_ref, tn_ref, g_ref, xo_ref, ho_ref, win_ref, *, nlat):
    tm = cc_ref.shape[1]
    nc = win_ref.shape[0]
    pvalid, nvalid = _halo_valid(nlat)

    off = HALO - SC_KERNEL // 2
    w = w_ref[...]
    chunks = []
    for c in range(nc):
        sl = slice(c * LANES, (c + 1) * LANES)

        def prod(c_ref, v_ref):
            return c_ref[0, :, sl].astype(F32) * v_ref[0, :, sl].astype(F32)

        win_ref[c, 0:HALO, :] = prod(pc_ref, pv_ref) * pvalid
        win_ref[c, HALO:HALO + tm, :] = prod(cc_ref, cv_ref)
        win_ref[c, HALO + tm:2 * HALO + tm, :] = prod(nc_ref, nv_ref) * nvalid
        y = win_ref[c, pl.ds(off, tm), :] * w[0:1, sl]
        for k in range(1, SC_KERNEL):
            y = y + win_ref[c, pl.ds(off + k, tm), :] * w[k:k + 1, sl]
        chunks.append(((bg_ref[0, :, sl].astype(F32) * y) * _silu(z_ref[0, :, sl].astype(F32))).astype(BF16))
    u = jnp.concatenate(chunks, axis=1)
    xo_ref[0], ho_ref[0] = _residual_out(u, wo_ref, x_ref[0], tg_ref[0, 0][2:3], tn_ref[0, 0], g_ref[...],
                                         k_major=True)


def _shortconv_layer(bcvz, p, w_out, x, tab, tab_next, norm_g_next, s, l):
    nb, r, d4 = bcvz.shape
    d = d4 // 4
    tm = ROW_TILE
    assert l == tm and s % tm == 0
    nlat = s // tm
    specs = [pl.BlockSpec((1, tm, d), lambda b, t: (b, t, 0))]
    specs += _halo_specs(tm, d, 1, r) + _halo_specs(tm, d, 2, r)
    specs.append(pl.BlockSpec((1, tm, d), lambda b, t: (b, t, 3)))
    specs.append(pl.BlockSpec((8, d), lambda b, t: (0, 0)))
    w = jnp.pad(p["conv_w"].astype(F32), ((0, 8 - SC_KERNEL), (0, 0)))
    o_in, o_args, out_specs, out_shape = _out_specs_and_args(w_out, x, tab, tab_next, norm_g_next, nlat)
    return pl.pallas_call(
        functools.partial(_sc_out_kernel, nlat=nlat),
        grid=(nb, r // tm),
        in_specs=specs + o_in,
        out_specs=out_specs,
        out_shape=out_shape,
        scratch_shapes=[pltpu.VMEM((d // LANES, tm + 2 * HALO, LANES), F32)],
        compiler_params=_cparams("parallel", "parallel"),
        name="shortconv_layer",
    )(bcvz, bcvz, bcvz, bcvz, bcvz, bcvz, bcvz, bcvz, w, *o_args)


def _forward(x, c, ctx, c_ctx, layers, final_norm_g):
    nb, s, d = x.shape
    l = ctx.shape[1]
    r = s + l
    assert l == ROW_TILE and s % ROW_TILE == 0 and nb < 8
    depth = len(layers)
    kinds = ("attn", "conformer", "shortconv")

    cond8 = jnp.zeros((8, d), F32).at[:nb].set(c).at[nb].set(c_ctx)
    tabs = [_mod_table(_adaln(cond8, p["ada_w"], p["ada_b"]), nb, d) for p in layers]
    rope = _rope_tables(s, l)

    h = _prologue(x, ctx, tabs[0], layers[0]["norm_g"])
    xs = None
    for i, p in enumerate(layers):
        kind = kinds[i % len(kinds)]
        last = i == depth - 1
        w_in = p["w_in"]
        w_out = p["w_out"].astype(BF16)
        h2 = h.reshape(nb * r, d)
        if kind == "attn":
            qkvz = _in_proj(h2, w_in, r, rope=rope, rope_cols=w_in.shape[1] // 2).reshape(nb, r, -1)
            u = _diff_attention(qkvz, p, i, s, l, keep_ctx=not last)
            if last:
                return _out_proj_final(u, w_out, xs, tabs[i], final_norm_g, s)
            assert i == 0, "attention layers sit at the two ends of the stack"
            xs, h = _out_proj_first(u, w_out, x, ctx, tabs[i], tabs[i + 1], layers[i + 1]["norm_g"])
        else:
            assert not last and i > 0, "the stack starts and ends with an attention layer"
            proj = _in_proj(h2, w_in, r).reshape(nb, r, -1)
            layer = _conformer_layer if kind == "conformer" else _shortconv_layer
            xs, h = layer(proj, p, w_out, xs, tabs[i], tabs[i + 1], layers[i + 1]["norm_g"], s, l)


def kernel(x, c, ctx, c_ctx, l0_norm_g, l0_ada_w, l0_ada_b, l0_w_in, l0_lam_q1, l0_lam_k1, l0_lam_q2, l0_lam_k2, l0_head_g, l0_w_out, l1_norm_g, l1_ada_w, l1_ada_b, l1_w_in, l1_dw_w, l1_dw_b, l1_ln_g, l1_ln_b, l1_w_out, l2_norm_g, l2_ada_w, l2_ada_b, l2_w_in, l2_conv_w, l2_w_out, l3_norm_g, l3_ada_w, l3_ada_b, l3_w_in, l3_lam_q1, l3_lam_k1, l3_lam_q2, l3_lam_k2, l3_head_g, l3_w_out, final_norm_g):
    layers = [
        dict(norm_g=l0_norm_g, ada_w=l0_ada_w, ada_b=l0_ada_b, w_in=l0_w_in, lam_q1=l0_lam_q1, lam_k1=l0_lam_k1,
             lam_q2=l0_lam_q2, lam_k2=l0_lam_k2, head_g=l0_head_g, w_out=l0_w_out),
        dict(norm_g=l1_norm_g, ada_w=l1_ada_w, ada_b=l1_ada_b, w_in=l1_w_in, dw_w=l1_dw_w, dw_b=l1_dw_b,
             ln_g=l1_ln_g, ln_b=l1_ln_b, w_out=l1_w_out),
        dict(norm_g=l2_norm_g, ada_w=l2_ada_w, ada_b=l2_ada_b, w_in=l2_w_in, conv_w=l2_conv_w, w_out=l2_w_out),
        dict(norm_g=l3_norm_g, ada_w=l3_ada_w, ada_b=l3_ada_b, w_in=l3_w_in, lam_q1=l3_lam_q1, lam_k1=l3_lam_k1,
             lam_q2=l3_lam_q2, lam_k2=l3_lam_k2, head_g=l3_head_g, w_out=l3_w_out),
    ]
    return _forward(x, c, ctx, c_ctx, layers, final_norm_g)
```
